```python
import jax, jax.numpy as jnp
from jax import lax
import numpy as np

D_MODEL = 2048
BATCH = 4
SEQ = 2048
DEPTH = 2

POOL_WIDTH = D_MODEL // 2
POOL_GROUPS = 4
POOL_GROUP_DIM = POOL_WIDTH // POOL_GROUPS
POOL_WINDOWS = (2, 4, 8, 16)
RWKV_WIDTH = D_MODEL // 2
RWKV_HEAD_DIM = 64
RWKV_HEADS = RWKV_WIDTH // RWKV_HEAD_DIM
DECAY_LORA = 64
AAA_LORA = 64
MV_LORA = 32
RMS_EPS = 1e-6
LNX_EPS = 64e-5
N_SHIFT_COLS = 4 * RWKV_WIDTH + DECAY_LORA + AAA_LORA
N_IN = 2 * POOL_WIDTH + N_SHIFT_COLS + 2 * D_MODEL

kernel_name = "hybrid_pool_rwkv7_adaln_sandwich"


def rmsnorm(x, g):
    xf = x.astype(jnp.float32)
    y = xf * lax.rsqrt(jnp.mean(xf * xf, axis=-1, keepdims=True) + RMS_EPS)
    return (y * g).astype(x.dtype)


def token_shift(z, mu):
    prev = jnp.pad(z, ((0, 0), (1, 0), (0, 0)))[:, :-1]
    return z + (prev - z) * mu


def causal_multiscale_pool(u):
    b, s, _ = u.shape
    ug = u.astype(jnp.float32).reshape(b, s, POOL_GROUPS, POOL_GROUP_DIM)
    cs = jnp.pad(jnp.cumsum(ug, axis=1), ((0, 0), (1, 0), (0, 0), (0, 0)))
    hi = jnp.arange(1, s + 1)
    outs = []
    for g, w in enumerate(POOL_WINDOWS):
        lo = jnp.maximum(hi - w, 0)
        cnt = (hi - lo).astype(jnp.float32)[None, :, None]
        mean = (cs[:, hi, g] - cs[:, lo, g]) / cnt
        outs.append(mean - ug[:, :, g])
    return jnp.stack(outs, axis=2).astype(u.dtype)


def wkv7(r, decay, k, v, a, b):
    bsz, _, h, n = r.shape

    def step(state, inp):
        r_t, w_t, k_t, v_t, a_t, b_t = inp
        sa = jnp.einsum('bhvk,bhk->bhv', state, a_t)
        state = (state * w_t[:, :, None, :] + sa[..., None] * b_t[:, :, None, :]
                 + v_t[..., None] * k_t[:, :, None, :])
        y = jnp.einsum('bhvk,bhk->bhv', state, r_t)
        return state, y

    xs = tuple(jnp.moveaxis(t.astype(jnp.float32), 1, 0) for t in (r, decay, k, v, a, b))
    s0 = jnp.zeros((bsz, h, n, n), jnp.float32)
    _, y = lax.scan(step, s0, xs)
    return jnp.moveaxis(y, 0, 1)


def setup_inputs(seed: int = 0) -> dict:
    key = jax.random.key(seed)
    ks = jax.random.split(key, 32)
    f32 = jnp.float32
    nrm = lambda k, shape, s: jax.random.normal(k, shape, f32) * s
    L, Lm = DEPTH, DEPTH - 1
    H, N = RWKV_HEADS, RWKV_HEAD_DIM
    return {
        "x": nrm(ks[0], (BATCH, SEQ, D_MODEL), 1.0),
        "c": nrm(ks[1], (BATCH, D_MODEL), 1.0),
        "w_ada": nrm(ks[2], (L, D_MODEL, 3 * D_MODEL), D_MODEL ** -0.5),
        "b_ada": nrm(ks[3], (L, 3 * D_MODEL), 0.02),
        "g_pre": 1.0 + nrm(ks[4], (L, D_MODEL), 0.05),
        "w_in": nrm(ks[5], (L, D_MODEL, N_IN), D_MODEL ** -0.5),
        "w_pool": nrm(ks[6], (L, POOL_GROUPS, POOL_GROUP_DIM, POOL_GROUP_DIM), POOL_GROUP_DIM ** -0.5),
        "pool_scale": jax.random.uniform(ks[7], (L, POOL_WIDTH), f32, 0.5, 1.5),
        "mu_shift": jax.random.uniform(ks[8], (L, N_SHIFT_COLS), f32, 0.0, 1.0),
        "w_decay_up": nrm(ks[9], (L, DECAY_LORA, RWKV_WIDTH), DECAY_LORA ** -0.5),
        "w0": jax.random.uniform(ks[10], (L, RWKV_WIDTH), f32, -3.0, 1.0),
        "w_aaa_up": nrm(ks[11], (L, AAA_LORA, RWKV_WIDTH), AAA_LORA ** -0.5),
        "a0": nrm(ks[12], (L, RWKV_WIDTH), 0.5),
        "w_mv_down": nrm(ks[13], (Lm, D_MODEL, MV_LORA), D_MODEL ** -0.5),
        "mu_mv": jax.random.uniform(ks[14], (Lm, MV_LORA), f32, 0.0, 1.0),
        "w_mv_up": nrm(ks[15], (Lm, MV_LORA, RWKV_WIDTH), MV_LORA ** -0.5),
        "mv0": nrm(ks[16], (Lm, RWKV_WIDTH), 0.5),
        "k_k": 0.85 + nrm(ks[17], (L, RWKV_WIDTH), 0.05),
        "k_a": 1.0 + nrm(ks[18], (L, RWKV_WIDTH), 0.05),
        "r_k": nrm(ks[19], (L, H, N), 0.1),
        "lnx_g": 1.0 + nrm(ks[20], (L, RWKV_WIDTH), 0.05),
        "lnx_b": nrm(ks[21], (L, RWKV_WIDTH), 0.02),
        "w_br_a": nrm(ks[22], (L, POOL_WIDTH, D_MODEL), POOL_WIDTH ** -0.5),
        "w_br_b": nrm(ks[23], (L, RWKV_WIDTH, D_MODEL), RWKV_WIDTH ** -0.5),
        "w_out": nrm(ks[24], (L, D_MODEL, D_MODEL), D_MODEL ** -0.5),
        "g_post": 1.0 + nrm(ks[25], (L, D_MODEL), 0.05),
    }


def reference(x, c, w_ada, b_ada, g_pre, w_in, w_pool, pool_scale, mu_shift,
              w_decay_up, w0, w_aaa_up, a0, w_mv_down, mu_mv, w_mv_up, mv0,
              k_k, k_a, r_k, lnx_g, lnx_b, w_br_a, w_br_b, w_out, g_post):
    bsz, s, _ = x.shape
    H, N = RWKV_HEADS, RWKV_HEAD_DIM
    cond = jax.nn.silu(c)
    split_in = [POOL_WIDTH, 2 * POOL_WIDTH, 2 * POOL_WIDTH + N_SHIFT_COLS,
                2 * POOL_WIDTH + N_SHIFT_COLS + D_MODEL]
    split_shift = [RWKV_WIDTH, 2 * RWKV_WIDTH, 3 * RWKV_WIDTH, 4 * RWKV_WIDTH,
                   4 * RWKV_WIDTH + DECAY_LORA]
    v_first = None
    for l in range(DEPTH):
        mod = cond @ w_ada[l] + b_ada[l]
        shift, scale, gate = jnp.split(mod, 3, axis=-1)
        h = rmsnorm(x, g_pre[l]) * (1.0 + scale[:, None]) + shift[:, None]

        proj = h @ w_in[l]
        u_a, z_a, shifted, gate_a, gate_b = jnp.split(proj, split_in, axis=-1)

        pooled = causal_multiscale_pool(u_a)
        y_a = jnp.einsum('bsgc,gcd->bsgd', pooled, w_pool[l]).reshape(bsz, s, POOL_WIDTH)
        y_a = y_a * pool_scale[l] * jax.nn.silu(z_a)

        xs = token_shift(shifted, mu_shift[l])
        r, k, v, z_b, w_lo, a_lo = jnp.split(xs, split_shift, axis=-1)
        w_log = -jax.nn.softplus(-(w0[l] + jnp.tanh(w_lo.astype(jnp.float32)) @ w_decay_up[l])) - 0.5
        decay = jnp.exp(-jnp.exp(w_log))
        a = jax.nn.sigmoid(a0[l] + a_lo @ w_aaa_up[l])
        if l == 0:
            v_first = v
        else:
            mv = token_shift(h @ w_mv_down[l - 1], mu_mv[l - 1])
            v = v + (v_first - v) * jax.nn.sigmoid(mv0[l - 1] + mv @ w_mv_up[l - 1])
        kk = (k * k_k[l]).reshape(bsz, s, H, N).astype(jnp.float32)
        kk = kk / jnp.maximum(jnp.linalg.norm(kk, axis=-1, keepdims=True), 1e-12)
        k = k * (1.0 + (a - 1.0) * k_a[l])
        rh, kh, vh, ah = (t.reshape(bsz, s, H, N) for t in (r, k, v, a))
        dh = decay.reshape(bsz, s, H, N)
        o = wkv7(rh, dh, kh, vh, -kk, kk * ah)
        mu_o = jnp.mean(o, axis=-1, keepdims=True)
        var_o = jnp.mean(jnp.square(o - mu_o), axis=-1, keepdims=True)
        o = ((o - mu_o) * lax.rsqrt(var_o + LNX_EPS)).reshape(bsz, s, RWKV_WIDTH)
        o = (o * lnx_g[l] + lnx_b[l]).astype(v.dtype)
        bonus = jnp.sum(rh * kh * r_k[l], axis=-1, keepdims=True) * vh
        y_b = (o + bonus.reshape(bsz, s, RWKV_WIDTH)) * jax.nn.silu(z_b)

        merged = (jax.nn.sigmoid(gate_a) * (y_a @ w_br_a[l])
                  + jax.nn.sigmoid(gate_b) * (y_b @ w_br_b[l]))
        out = merged @ w_out[l]

        x = x + gate[:, None] * rmsnorm(out, g_post[l])
    return x
```

```python
import functools

import jax
import jax.numpy as jnp
from jax import lax
from jax.experimental import pallas as pl
from jax.experimental.pallas import tpu as pltpu

F32 = jnp.float32
BF16 = jnp.bfloat16

RMS_EPS = 1e-6
LNX_EPS = 64e-5
POOL_WINDOWS = (2, 4, 8, 16)
HEAD_DIM = 64
DECAY_LORA = 64
AAA_LORA = 64
MV_LORA = 32

CHUNK = 64
HEADS_PER_GROUP = 4
GROUP_W = HEADS_PER_GROUP * HEAD_DIM
SMALL_W = 256

_MIB = 1024 * 1024
_V7X_VMEM_LIMIT = 56 * _MIB

_NT = (((1,), (1,)), ((), ()))
_TN = (((0,), (0,)), ((), ()))


def _params(*sem):
    return pltpu.CompilerParams(dimension_semantics=sem, vmem_limit_bytes=_V7X_VMEM_LIMIT)


def _dot(a, b):
    return jnp.dot(a, b, preferred_element_type=F32)


def _bf(x):
    return x.astype(BF16)


def _mod_kernel(c_ref, w_ref, b_ref, o_ref):
    cond = _bf(jax.nn.silu(c_ref[...]))
    o_ref[...] = _dot(cond, _bf(w_ref[...])) + b_ref[...]


def _mod_call(c8, w_ada, b_ada3):
    depth, d, n = w_ada.shape
    tn = 512
    return pl.pallas_call(
        _mod_kernel,
        grid=(depth, n // tn),
        in_specs=[
            pl.BlockSpec((8, d), lambda l, j: (0, 0)),
            pl.BlockSpec((None, d, tn), lambda l, j: (l, 0, j)),
            pl.BlockSpec((None, 1, tn), lambda l, j: (l, 0, j)),
        ],
        out_specs=pl.BlockSpec((None, 8, tn), lambda l, j: (l, 0, j)),
        out_shape=jax.ShapeDtypeStruct((depth, 8, n), F32),
        compiler_params=_params("parallel", "parallel"),
        name="adaln_mod",
    )(c8, w_ada, b_ada3)


def _prenorm_kernel(x_ref, g_ref, sc_ref, sh_ref, ws_ref, h_ref, sm_ref):
    x = x_ref[...]
    y = x * lax.rsqrt(jnp.mean(x * x, axis=-1, keepdims=True) + RMS_EPS)
    h = _bf((y * g_ref[...]) * (1.0 + sc_ref[...]) + sh_ref[...])
    h_ref[...] = h
    sm_ref[...] = _dot(h, ws_ref[...])


def _prenorm_call(xf, g, scale, shift, w_small, seq):
    t, d = xf.shape
    tm = 512
    per_seq = seq // tm
    return pl.pallas_call(
        _prenorm_kernel,
        grid=(t // tm,),
        in_specs=[
            pl.BlockSpec((tm, d), lambda i: (i, 0)),
            pl.BlockSpec((1, d), lambda i: (0, 0)),
            pl.BlockSpec((None, 1, d), lambda i: (i // per_seq, 0, 0)),
            pl.BlockSpec((None, 1, d), lambda i: (i // per_seq, 0, 0)),
            pl.BlockSpec((d, SMALL_W), lambda i: (0, 0)),
        ],
        out_specs=[
            pl.BlockSpec((tm, d), lambda i: (i, 0)),
            pl.BlockSpec((tm, SMALL_W), lambda i: (i, 0)),
        ],
        out_shape=[
            jax.ShapeDtypeStruct((t, d), BF16),
            jax.ShapeDtypeStruct((t, SMALL_W), F32),
        ],
        compiler_params=_params("parallel"),
        name="prenorm",
    )(xf, g, scale, shift, w_small)


def _matmul_kernel(a_ref, b_ref, o_ref):
    o_ref[...] = _dot(a_ref[...], b_ref[...]).astype(o_ref.dtype)


def _matmul_call(a, b, out_dtype, tm, tn):
    m, k = a.shape
    n = b.shape[1]
    tm = min(tm, m)
    return pl.pallas_call(
        _matmul_kernel,
        grid=(m // tm, n // tn),
        in_specs=[
            pl.BlockSpec((tm, k), lambda i, j: (i, 0)),
            pl.BlockSpec((k, tn), lambda i, j: (0, j)),
        ],
        out_specs=pl.BlockSpec((tm, tn), lambda i, j: (i, j)),
        out_shape=jax.ShapeDtypeStruct((m, n), out_dtype),
        compiler_params=_params("parallel", "parallel"),
        name="in_proj",
    )(a, b)


def _pool_kernel(u_ref, z_ref, w_ref, ps_ref, y_ref):
    group = pl.program_id(1)
    u = u_ref[...]
    row = lax.broadcasted_iota(jnp.int32, (u.shape[0], 1), 0)

    def shifted(a, s):
        return jnp.where(row >= s, pltpu.roll(a, s, 0), 0.0)

    for gi, window in enumerate(POOL_WINDOWS):

        @pl.when(group == gi)
        def _():
            acc, span = u, 1
            while span < window:
                acc = acc + shifted(acc, span)
                span *= 2
            cnt = jnp.minimum(row + 1, window).astype(F32)
            pooled = acc / cnt - u
            y = _dot(_bf(pooled), w_ref[...])
            y_ref[...] = _bf(y * ps_ref[...] * jax.nn.silu(z_ref[...]))


def _pool_call(proj3, w_pool, pool_scale):
    b, s, _ = proj3.shape
    groups, gd, _ = w_pool.shape
    return pl.pallas_call(
        _pool_kernel,
        grid=(b, groups),
        in_specs=[
            pl.BlockSpec((None, s, gd), lambda i, g: (i, 0, g)),
            pl.BlockSpec((None, s, gd), lambda i, g: (i, 0, groups + g)),
            pl.BlockSpec((None, gd, gd), lambda i, g: (g, 0, 0)),
            pl.BlockSpec((1, gd), lambda i, g: (0, g)),
        ],
        out_specs=pl.BlockSpec((None, s, gd), lambda i, g: (i, 0, g)),
        out_shape=jax.ShapeDtypeStruct((b, s, groups * gd), BF16),
        compiler_params=_params("parallel", "parallel"),
        name="pool_mixer",
    )(proj3, proj3, w_pool, pool_scale)


(_P_MU_R, _P_MU_K, _P_MU_V, _P_MU_Z, _P_W0, _P_A0, _P_MV0, _P_KK, _P_KA, _P_RK,
 _P_LNG, _P_LNB) = range(12)
_P_ROWS = 16


def _split_terms(x, terms):
    pieces, rem = [], x
    for _ in range(terms):
        p = _bf(rem)
        pieces.append(p)
        rem = rem - p.astype(F32)
    return pieces


def _rwkv_kernel(has_mix, *refs):
    if has_mix:
        (r_ref, k_ref, v_ref, z_ref, sm_ref, vf_ref, pv_ref, mus_ref, wl_ref,
         y_ref, s_scr) = refs
    else:
        (r_ref, k_ref, v_ref, z_ref, sm_ref, pv_ref, mus_ref, wl_ref,
         y_ref, vout_ref, s_scr) = refs
    seq = r_ref.shape[0]
    c_, w_, st = CHUNK, GROUP_W, HEADS_PER_GROUP * CHUNK

    pv = pv_ref[...]
    prow = lambda i: pv[i:i + 1, :]
    mus = mus_ref[...]
    wl = wl_ref[...]

    def iota(shape, dim):
        return lax.broadcasted_iota(jnp.int32, shape, dim)

    first_row = iota((c_, w_), 0) == 0
    lane = iota((c_, w_), 1)
    m_wlo = lane < DECAY_LORA
    m_alo = (lane >= DECAY_LORA) & (lane < DECAY_LORA + AAA_LORA)
    m_mv = (lane >= DECAY_LORA + AAA_LORA) & (lane < DECAY_LORA + AAA_LORA + MV_LORA)
    head_masks = [lane // HEAD_DIM == j for j in range(HEADS_PER_GROUP)]
    sq_r, sq_c = iota((st, w_), 0), iota((st, w_), 1)
    blockdiag = sq_r // HEAD_DIM == sq_c // HEAD_DIM
    ones_bd = jnp.where(blockdiag, 1.0, 0.0).astype(BF16)
    strict_bd = sq_r % c_ > sq_c % c_
    eye = jnp.where(sq_r == sq_c, 1.0, 0.0).astype(F32)
    ga_mask = iota((st, c_), 0) % c_ > iota((st, c_), 1)
    gr_mask = iota((st, 2 * c_), 0) % c_ >= iota((st, 2 * c_), 1) % c_
    tri = jnp.where(iota((c_, c_), 0) >= iota((c_, c_), 1), 1.0, 0.0).astype(BF16)

    def segsum(x):
        hi, lo = _split_terms(x, 2)
        return _dot(hi, ones_bd) + _dot(lo, ones_bd)

    def stack(x):
        return _bf(jnp.concatenate([jnp.where(m, x, 0.0) for m in head_masks], axis=0))

    def collapse(x):
        return x[0:c_] + x[c_:2 * c_] + x[2 * c_:3 * c_] + x[3 * c_:4 * c_]

    s_scr[...] = jnp.zeros_like(s_scr)

    def body(c, carry):
        r0 = pl.multiple_of(c * c_, c_)
        rows = pl.ds(r0, c_)
        prev_idx = jnp.maximum(r0 - 1, 0)
        not_first = (c > 0).astype(F32)

        def tshift(ref, mu):
            cur = ref[rows, :]
            last = ref[pl.ds(prev_idx, 1), :] * not_first
            prev = jnp.where(first_row, last, pltpu.roll(cur, 1, 0))
            return cur + (prev - cur) * mu

        xr = tshift(r_ref, prow(_P_MU_R))
        xk = tshift(k_ref, prow(_P_MU_K))
        xv = tshift(v_ref, prow(_P_MU_V))
        xz = tshift(z_ref, prow(_P_MU_Z))
        sms = tshift(sm_ref, mus)

        lw = _dot(_bf(jnp.where(m_wlo, jnp.tanh(sms), 0.0)), wl)
        la = _dot(_bf(jnp.where(m_alo, sms, 0.0)), wl)
        w_log = -jax.nn.softplus(-(prow(_P_W0) + lw)) - 0.5
        ld = -jnp.exp(w_log)
        asig = jax.nn.sigmoid(prow(_P_A0) + la)
        if has_mix:
            lm = _dot(_bf(jnp.where(m_mv, sms, 0.0)), wl)
            xv = xv + (vf_ref[rows, :] - xv) * jax.nn.sigmoid(prow(_P_MV0) + lm)
        else:
            vout_ref[rows, :] = xv
        kk = xk * prow(_P_KK)
        kkn = kk / jnp.maximum(jnp.sqrt(segsum(kk * kk)), 1e-12)
        kp = xk * (1.0 + (asig - 1.0) * prow(_P_KA))
        a = -kkn
        b = kkn * asig
        bonus = segsum(xr * kp * prow(_P_RK)) * xv

        cum = sum(_dot(tri, p) for p in _split_terms(ld, 3))
        last = cum[c_ - 1:c_, :]
        e_in = jnp.exp(cum)
        e_out = jnp.exp(-cum)
        e_end = jnp.exp(last - cum)
        w_end = jnp.exp(last)
        a_t = a * jnp.exp(cum - ld)
        r_t = xr * e_in
        b_t = b * e_out
        k_t = kp * e_out
        b_c = _bf(b * e_end)
        k_c = _bf(kp * e_end)
        vb = _bf(xv)

        a_st, r_st, b_st = stack(a_t), stack(r_t), stack(b_t)
        k_tb, b_tb = _bf(k_t), _bf(b_t)
        a_bd = jnp.where(strict_bd, lax.dot_general(a_st, b_st, _NT, preferred_element_type=F32), 0.0)
        g_a = jnp.where(ga_mask, lax.dot_general(a_st, k_tb, _NT, preferred_element_type=F32), 0.0)
        g_r = jnp.where(
            gr_mask,
            lax.dot_general(r_st, jnp.concatenate([b_tb, k_tb], axis=0), _NT,
                            preferred_element_type=F32), 0.0)
        akv = jnp.where(blockdiag, _dot(_bf(g_a), vb), 0.0)

        p = _bf(a_bd)
        t_inv = eye + a_bd
        span = 2
        while span < c_:
            p2 = _dot(p, p)
            p = _bf(p2)
            t_inv = t_inv + _dot(_bf(t_inv), p)
            span *= 2

        x1 = _dot(_bf(t_inv), jnp.concatenate([a_st, _bf(akv)], axis=1))
        a_p = collapse(x1[:, :w_])
        u0 = collapse(x1[:, w_:])
        a_pb, u0b = _bf(a_p), _bf(u0)
        rhs = jnp.concatenate(
            [jnp.concatenate([a_pb, u0b], axis=1),
             jnp.concatenate([jnp.zeros_like(vb), vb], axis=1)], axis=0)
        x2 = _dot(_bf(g_r), rhs)
        r_p = r_t + collapse(jnp.where(blockdiag, x2[:, :w_], 0.0))
        y0 = collapse(jnp.where(blockdiag, x2[:, w_:], 0.0))

        m_p = jnp.where(blockdiag, lax.dot_general(a_pb, b_c, _TN, preferred_element_type=F32), 0.0)
        n_n = jnp.where(
            blockdiag,
            lax.dot_general(jnp.concatenate([u0b, vb], axis=0),
                            jnp.concatenate([b_c, k_c], axis=0), _TN,
                            preferred_element_type=F32), 0.0)

        state = s_scr[...]
        state_b = _bf(state)
        o = lax.dot_general(_bf(r_p), state_b, _NT, preferred_element_type=F32) + y0
        s_scr[...] = state * w_end + _dot(state_b, _bf(m_p)) + n_n

        mu_o = segsum(o) * (1.0 / HEAD_DIM)
        d = o - mu_o
        var = segsum(d * d) * (1.0 / HEAD_DIM)
        o = d * lax.rsqrt(var + LNX_EPS) * prow(_P_LNG) + prow(_P_LNB)
        y_ref[rows, :] = _bf((o + bonus) * jax.nn.silu(xz))
        return carry

    lax.fori_loop(0, seq // c_, body, 0)


def _rwkv_call(proj3, small3, v_first, pvec, mu_small, w_lora):
    b, s, _ = proj3.shape
    rw = pvec.shape[1]
    groups = rw // GROUP_W
    has_mix = v_first is not None
    col0 = 2 * rw // GROUP_W

    def col_spec(offset):
        return pl.BlockSpec((None, s, GROUP_W), lambda i, g: (i, 0, offset + g))

    in_specs = [col_spec(col0), col_spec(col0 + groups), col_spec(col0 + 2 * groups),
                col_spec(col0 + 3 * groups),
                pl.BlockSpec((None, s, SMALL_W), lambda i, g: (i, 0, 0))]
    args = [proj3, proj3, proj3, proj3, small3]
    if has_mix:
        in_specs.append(col_spec(0))
        args.append(v_first)
    in_specs += [
        pl.BlockSpec((_P_ROWS, GROUP_W), lambda i, g: (0, g)),
        pl.BlockSpec((1, SMALL_W), lambda i, g: (0, 0)),
        pl.BlockSpec((SMALL_W, GROUP_W), lambda i, g: (0, g)),
    ]
    args += [pvec, mu_small, w_lora]
    out_specs = [col_spec(0)]
    out_shape = [jax.ShapeDtypeStruct((b, s, rw), BF16)]
    if not has_mix:
        out_specs.append(col_spec(0))
        out_shape.append(jax.ShapeDtypeStruct((b, s, rw), F32))
    res = pl.pallas_call(
        functools.partial(_rwkv_kernel, has_mix),
        grid=(b, groups),
        in_specs=in_specs,
        out_specs=out_specs,
        out_shape=out_shape,
        scratch_shapes=[pltpu.VMEM((GROUP_W, GROUP_W), F32)],
        compiler_params=_params("parallel", "parallel"),
        name="rwkv_mixer",
    )(*args)
    return (res[0], v_first) if has_mix else (res[0], res[1])


def _merge_kernel(h_ref, ya_ref, yb_ref, wga_ref, wgb_ref, wa_ref, wb_ref, o_ref):
    h = h_ref[...]
    ga = jax.nn.sigmoid(_dot(h, wga_ref[...]))
    gb = jax.nn.sigmoid(_dot(h, wgb_ref[...]))
    pa = _dot(ya_ref[...], wa_ref[...])
    pb = _dot(yb_ref[...], wb_ref[...])
    o_ref[...] = _bf(ga * pa + gb * pb)


def _merge_call(h, ya, yb, w_gate, w_br_a, w_br_b):
    t, d = h.shape
    ka, kb = ya.shape[1], yb.shape[1]
    tm, tn = min(1024, t), 512
    nj = d // tn
    return pl.pallas_call(
        _merge_kernel,
        grid=(t // tm, nj),
        in_specs=[
            pl.BlockSpec((tm, d), lambda i, j: (i, 0)),
            pl.BlockSpec((tm, ka), lambda i, j: (i, 0)),
            pl.BlockSpec((tm, kb), lambda i, j: (i, 0)),
            pl.BlockSpec((d, tn), lambda i, j: (0, j)),
            pl.BlockSpec((d, tn), lambda i, j: (0, nj + j)),
            pl.BlockSpec((ka, tn), lambda i, j: (0, j)),
            pl.BlockSpec((kb, tn), lambda i, j: (0, j)),
        ],
        out_specs=pl.BlockSpec((tm, tn), lambda i, j: (i, j)),
        out_shape=jax.ShapeDtypeStruct((t, d), BF16),
        compiler_params=_params("parallel", "parallel"),
        name="gated_merge",
    )(h, ya, yb, w_gate, w_gate, w_br_a, w_br_b)


def _out_kernel(m_ref, w_ref, x_ref, gate_ref, g_ref, o_ref):
    out = _dot(m_ref[...], w_ref[...])
    y = out * lax.rsqrt(jnp.mean(out * out, axis=-1, keepdims=True) + RMS_EPS)
    o_ref[...] = x_ref[...] + gate_ref[...] * (y * g_ref[...])


def _out_call(merged, w_out, xf, gate, g_post, seq):
    t, d = xf.shape
    tm = 512
    per_seq = seq // tm
    return pl.pallas_call(
        _out_kernel,
        grid=(t // tm,),
        in_specs=[
            pl.BlockSpec((tm, d), lambda i: (i, 0)),
            pl.BlockSpec((d, d), lambda i: (0, 0)),
            pl.BlockSpec((tm, d), lambda i: (i, 0)),
            pl.BlockSpec((None, 1, d), lambda i: (i // per_seq, 0, 0)),
            pl.BlockSpec((1, d), lambda i: (0, 0)),
        ],
        out_specs=pl.BlockSpec((tm, d), lambda i: (i, 0)),
        out_shape=jax.ShapeDtypeStruct((t, d), F32),
        compiler_params=_params("parallel"),
        name="out_proj",
    )(merged, w_out, xf, gate, g_post)


def kernel(x, c, w_ada, b_ada, g_pre, w_in, w_pool, pool_scale, mu_shift, w_decay_up, w0, w_aaa_up, a0, w_mv_down, mu_mv, w_mv_up, mv0, k_k, k_a, r_k, lnx_g, lnx_b, w_br_a, w_br_b, w_out, g_post):
    bsz, seq, d = x.shape
    depth = w_ada.shape[0]
    pw = pool_scale.shape[1]
    rw = w0.shape[1]
    t = bsz * seq
    n_main = 2 * pw + 4 * rw
    n_lora = DECAY_LORA + AAA_LORA
    n_shift = 4 * rw + n_lora

    xf = x.reshape(t, d)
    c8 = jnp.pad(c, ((0, 8 - bsz), (0, 0)))
    mod = _mod_call(c8, w_ada, b_ada.reshape(depth, 1, 3 * d))

    v_first = None
    for l in range(depth):
        shift = mod[l, :bsz, :d].reshape(bsz, 1, d)
        scale = mod[l, :bsz, d:2 * d].reshape(bsz, 1, d)
        gate = mod[l, :bsz, 2 * d:].reshape(bsz, 1, d)

        w_main = _bf(w_in[l, :, :n_main])
        w_gate = _bf(w_in[l, :, n_main + n_lora:])
        small_cols = [w_in[l, :, n_main:n_main + n_lora]]
        lora_rows = [w_decay_up[l], w_aaa_up[l]]
        mu_small = [mu_shift[l, 4 * rw:n_shift]]
        mv0_l = jnp.zeros((rw,), F32)
        if l > 0:
            small_cols.append(w_mv_down[l - 1])
            lora_rows.append(w_mv_up[l - 1])
            mu_small.append(mu_mv[l - 1])
            mv0_l = mv0[l - 1]
        w_small = jnp.concatenate(small_cols, axis=1)
        w_small = _bf(jnp.pad(w_small, ((0, 0), (0, SMALL_W - w_small.shape[1]))))
        w_lora = jnp.concatenate(lora_rows, axis=0)
        w_lora = _bf(jnp.pad(w_lora, ((0, SMALL_W - w_lora.shape[0]), (0, 0))))
        mu_small = jnp.concatenate(mu_small)
        mu_small = jnp.pad(mu_small, (0, SMALL_W - mu_small.shape[0])).reshape(1, SMALL_W)
        mu4 = mu_shift[l, :4 * rw].reshape(4, rw)
        pvec = jnp.concatenate(
            [mu4, jnp.stack([w0[l], a0[l], mv0_l, k_k[l], k_a[l], r_k[l].reshape(rw),
                             lnx_g[l], lnx_b[l]]),
             jnp.zeros((_P_ROWS - 12, rw), F32)], axis=0)

        h, small = _prenorm_call(xf, g_pre[l].reshape(1, d), scale, shift, w_small, seq)
        proj = _matmul_call(h, w_main, F32, 1024, 1024)
        proj3 = proj.reshape(bsz, seq, n_main)
        small3 = small.reshape(bsz, seq, SMALL_W)

        y_a = _pool_call(proj3, _bf(w_pool[l]), pool_scale[l].reshape(1, pw))
        y_b, v_first = _rwkv_call(proj3, small3, v_first, pvec, mu_small, w_lora)

        merged = _merge_call(h, y_a.reshape(t, pw), y_b.reshape(t, rw), w_gate,
                             _bf(w_br_a[l]), _bf(w_br_b[l]))
        xf = _out_call(merged, _bf(w_out[l]), xf, gate, g_post[l].reshape(1, d), seq)
    return xf.reshape(bsz, seq, d)
```

```python
import functools

import jax
import jax.numpy as jnp
from jax import lax
from jax.experimental import pallas as pl
from jax.experimental.pallas import tpu as pltpu

F32 = jnp.float32
BF16 = jnp.bfloat16

RMS_EPS = 1e-6
LNX_EPS = 64e-5
POOL_WINDOWS = (2, 4, 8, 16)
HEAD_DIM = 64
DECAY_LORA = 64
AAA_LORA = 64
MV_LORA = 32

CHUNK = 64
PAIR_HEADS = 2
PAIR_W = PAIR_HEADS * HEAD_DIM
RWKV_BLOCK_W = 1024
RWKV_SEQ_TILE = 512
SMALL_W = 256

_MIB = 1024 * 1024
_V7X_VMEM_LIMIT = 56 * _MIB

_NT = (((1,), (1,)), ((), ()))
_TN = (((0,), (0,)), ((), ()))


def _params(*sem):
    return pltpu.CompilerParams(dimension_semantics=sem, vmem_limit_bytes=_V7X_VMEM_LIMIT)


def _dot(a, b):
    return jnp.dot(a, b, preferred_element_type=F32)


def _bf(x):
    return x.astype(BF16)


def _mod_kernel(c_ref, w_ref, b_ref, o_ref):
    cond = _bf(jax.nn.silu(c_ref[...]))
    o_ref[...] = _dot(cond, _bf(w_ref[...])) + b_ref[...]


def _mod_call(c8, w_ada, b_ada3):
    depth, d, n = w_ada.shape
    tn = 512
    return pl.pallas_call(
        _mod_kernel,
        grid=(depth, n // tn),
        in_specs=[
            pl.BlockSpec((8, d), lambda l, j: (0, 0)),
            pl.BlockSpec((None, d, tn), lambda l, j: (l, 0, j)),
            pl.BlockSpec((None, 1, tn), lambda l, j: (l, 0, j)),
        ],
        out_specs=pl.BlockSpec((None, 8, tn), lambda l, j: (l, 0, j)),
        out_shape=jax.ShapeDtypeStruct((depth, 8, n), F32),
        compiler_params=_params("parallel", "parallel"),
        name="adaln_mod",
    )(c8, w_ada, b_ada3)


def _prenorm_kernel(x_ref, g_ref, sc_ref, sh_ref, ws_ref, h_ref, sm_ref):
    x = x_ref[...]
    y = x * lax.rsqrt(jnp.mean(x * x, axis=-1, keepdims=True) + RMS_EPS)
    h = _bf((y * g_ref[...]) * (1.0 + sc_ref[...]) + sh_ref[...])
    h_ref[...] = h
    sm_ref[...] = _dot(h, ws_ref[...])


def _prenorm_call(xf, g, scale, shift, w_small, seq):
    t, d = xf.shape
    tm = 512
    per_seq = seq // tm
    return pl.pallas_call(
        _prenorm_kernel,
        grid=(t // tm,),
        in_specs=[
            pl.BlockSpec((tm, d), lambda i: (i, 0)),
            pl.BlockSpec((1, d), lambda i: (0, 0)),
            pl.BlockSpec((None, 1, d), lambda i: (i // per_seq, 0, 0)),
            pl.BlockSpec((None, 1, d), lambda i: (i // per_seq, 0, 0)),
            pl.BlockSpec((d, SMALL_W), lambda i: (0, 0)),
        ],
        out_specs=[
            pl.BlockSpec((tm, d), lambda i: (i, 0)),
            pl.BlockSpec((tm, SMALL_W), lambda i: (i, 0)),
        ],
        out_shape=[
            jax.ShapeDtypeStruct((t, d), BF16),
            jax.ShapeDtypeStruct((t, SMALL_W), F32),
        ],
        compiler_params=_params("parallel"),
        name="prenorm",
    )(xf, g, scale, shift, w_small)


def _matmul_kernel(a_ref, b_ref, o_ref):
    o_ref[...] = _dot(a_ref[...], b_ref[...]).astype(o_ref.dtype)


def _matmul_call(a, b, out_dtype, tm, tn):
    m, k = a.shape
    n = b.shape[1]
    tm = min(tm, m)
    return pl.pallas_call(
        _matmul_kernel,
        grid=(m // tm, n // tn),
        in_specs=[
            pl.BlockSpec((tm, k), lambda i, j: (i, 0)),
            pl.BlockSpec((k, tn), lambda i, j: (0, j)),
        ],
        out_specs=pl.BlockSpec((tm, tn), lambda i, j: (i, j)),
        out_shape=jax.ShapeDtypeStruct((m, n), out_dtype),
        compiler_params=_params("parallel", "parallel"),
        name="in_proj",
    )(a, b)


def _pool_kernel(u_ref, z_ref, w_ref, ps_ref, y_ref):
    group = pl.program_id(1)
    u = u_ref[...]
    row = lax.broadcasted_iota(jnp.int32, (u.shape[0], 1), 0)

    def shifted(a, s):
        return jnp.where(row >= s, pltpu.roll(a, s, 0), 0.0)

    for gi, window in enumerate(POOL_WINDOWS):

        @pl.when(group == gi)
        def _():
            acc, span = u, 1
            while span < window:
                acc = acc + shifted(acc, span)
                span *= 2
            cnt = jnp.minimum(row + 1, window).astype(F32)
            pooled = acc / cnt - u
            y = _dot(_bf(pooled), w_ref[...])
            y_ref[...] = _bf(y * ps_ref[...] * jax.nn.silu(z_ref[...]))


def _pool_call(proj3, w_pool, pool_scale):
    b, s, _ = proj3.shape
    groups, gd, _ = w_pool.shape
    return pl.pallas_call(
        _pool_kernel,
        grid=(b, groups),
        in_specs=[
            pl.BlockSpec((None, s, gd), lambda i, g: (i, 0, g)),
            pl.BlockSpec((None, s, gd), lambda i, g: (i, 0, groups + g)),
            pl.BlockSpec((None, gd, gd), lambda i, g: (g, 0, 0)),
            pl.BlockSpec((1, gd), lambda i, g: (0, g)),
        ],
        out_specs=pl.BlockSpec((None, s, gd), lambda i, g: (i, 0, g)),
        out_shape=jax.ShapeDtypeStruct((b, s, groups * gd), BF16),
        compiler_params=_params("parallel", "parallel"),
        name="pool_mixer",
    )(proj3, proj3, w_pool, pool_scale)


(_P_MU_R, _P_MU_K, _P_MU_V, _P_MU_Z, _P_W0, _P_A0, _P_MV0, _P_KK, _P_KA, _P_RK,
 _P_LNG, _P_LNB) = range(12)
_P_ROWS = 16


def _split_terms(x, terms):
    pieces, rem = [], x
    for _ in range(terms):
        p = _bf(rem)
        pieces.append(p)
        rem = rem - p.astype(F32)
    return pieces


def _rwkv_kernel(has_mix, *refs):
    if has_mix:
        (r_ref, k_ref, v_ref, z_ref, sm_ref, vf_ref, pv_ref, mus_ref, wl_ref,
         y_ref, state_scr, carry_scr, carry_sm_scr) = refs
    else:
        (r_ref, k_ref, v_ref, z_ref, sm_ref, pv_ref, mus_ref, wl_ref,
         y_ref, vout_ref, state_scr, carry_scr, carry_sm_scr) = refs
    ts, wb = r_ref.shape
    c_, pw, st = CHUNK, PAIR_W, PAIR_HEADS * CHUNK
    shifted_refs = (r_ref, k_ref, v_ref, z_ref)

    @pl.when(pl.program_id(2) == 0)
    def _():
        state_scr[...] = jnp.zeros_like(state_scr)
        carry_scr[...] = jnp.zeros_like(carry_scr)
        carry_sm_scr[...] = jnp.zeros_like(carry_sm_scr)

    pv = pv_ref[...]
    mus = mus_ref[...]

    def iota(shape, dim):
        return lax.broadcasted_iota(jnp.int32, shape, dim)

    first_row = iota((c_, pw), 0) == 0
    first_row_sm = iota((c_, SMALL_W), 0) == 0
    lane_sm = iota((c_, SMALL_W), 1)
    m_wlo = lane_sm < DECAY_LORA
    m_alo = (lane_sm >= DECAY_LORA) & (lane_sm < DECAY_LORA + AAA_LORA)
    m_mv = (lane_sm >= DECAY_LORA + AAA_LORA) & (lane_sm < DECAY_LORA + AAA_LORA + MV_LORA)
    lane = iota((c_, pw), 1)
    head_masks = [lane // HEAD_DIM == j for j in range(PAIR_HEADS)]
    sq_r, sq_c = iota((st, pw), 0), iota((st, pw), 1)
    blockdiag = sq_r // HEAD_DIM == sq_c // HEAD_DIM
    ones_bd = jnp.where(blockdiag, 1.0, 0.0).astype(BF16)
    strict_bd = sq_r % c_ > sq_c % c_
    eye = jnp.where(sq_r == sq_c, 1.0, 0.0).astype(F32)
    ga_mask = iota((st, c_), 0) % c_ > iota((st, c_), 1)
    gr_mask = iota((st, 2 * c_), 0) % c_ >= iota((st, 2 * c_), 1) % c_
    tri = jnp.where(iota((c_, c_), 0) >= iota((c_, c_), 1), 1.0, 0.0).astype(BF16)

    def segsum(x):
        hi, lo = _split_terms(x, 2)
        return _dot(hi, ones_bd) + _dot(lo, ones_bd)

    def stack(x):
        return _bf(jnp.concatenate([jnp.where(m, x, 0.0) for m in head_masks], axis=0))

    def collapse(x):
        return x[0:c_] + x[c_:2 * c_]

    def load_pair(c, p):
        lanes = slice(p * pw, (p + 1) * pw)
        r0 = pl.multiple_of(c * c_, c_)
        rows = pl.ds(r0, c_)
        prev_rows = pl.ds(pl.multiple_of(jnp.maximum(r0 - 8, 0), 8), 8)
        cur = [ref[rows, lanes] for ref in shifted_refs]
        last = [jnp.where(c > 0, ref[prev_rows, lanes], carry_scr[i, :, lanes])[7:8]
                for i, ref in enumerate(shifted_refs)]
        vf = vf_ref[rows, lanes] if has_mix else None
        return cur, last, vf, wl_ref[:, lanes], state_scr[p]

    def pair_chunk(p, loaded, lhs_w, lhs_a, lhs_m):
        lanes = slice(p * pw, (p + 1) * pw)
        cur, last, vf, wl, state = loaded
        prow = lambda i: pv[i:i + 1, lanes]

        def tshift(i, mu):
            prev = jnp.where(first_row, last[i], pltpu.roll(cur[i], 1, 0))
            return cur[i] + (prev - cur[i]) * mu

        xr = tshift(0, prow(_P_MU_R))
        xk = tshift(1, prow(_P_MU_K))
        xv = tshift(2, prow(_P_MU_V))
        xz = tshift(3, prow(_P_MU_Z))
        v_shifted = xv

        lw, la = _dot(lhs_w, wl), _dot(lhs_a, wl)
        lm = _dot(lhs_m, wl) if has_mix else None
        yield
        w_log = -jax.nn.softplus(-(prow(_P_W0) + lw)) - 0.5
        ld = -jnp.exp(w_log)
        asig = jax.nn.sigmoid(prow(_P_A0) + la)
        if has_mix:
            xv = xv + (vf - xv) * jax.nn.sigmoid(prow(_P_MV0) + lm)
        kk = xk * prow(_P_KK)
        kp = xk * (1.0 + (asig - 1.0) * prow(_P_KA))
        kk_sq = segsum(kk * kk)
        rk_sum = segsum(xr * kp * prow(_P_RK))
        cum = sum(_dot(tri, piece) for piece in _split_terms(ld, 3))
        yield
        kkn = kk / jnp.maximum(jnp.sqrt(kk_sq), 1e-12)
        a = -kkn
        b = kkn * asig
        bonus = rk_sum * xv
        end = cum[c_ - 1:c_, :]
        e_in = jnp.exp(cum)
        e_out = jnp.exp(-cum)
        e_end = jnp.exp(end - cum)
        w_end = jnp.exp(end)
        a_t = a * jnp.exp(cum - ld)
        r_t = xr * e_in
        b_t = b * e_out
        k_t = kp * e_out
        b_c = _bf(b * e_end)
        k_c = _bf(kp * e_end)
        vb = _bf(xv)

        a_st, r_st, b_st = stack(a_t), stack(r_t), stack(b_t)
        k_tb, b_tb = _bf(k_t), _bf(b_t)
        a_bd = jnp.where(strict_bd, lax.dot_general(a_st, b_st, _NT, preferred_element_type=F32), 0.0)
        g_a = jnp.where(ga_mask, lax.dot_general(a_st, k_tb, _NT, preferred_element_type=F32), 0.0)
        g_r = jnp.where(
            gr_mask,
            lax.dot_general(r_st, jnp.concatenate([b_tb, k_tb], axis=0), _NT,
                            preferred_element_type=F32), 0.0)
        yield

        akv = jnp.where(blockdiag, _dot(_bf(g_a), vb), 0.0)
        a_pow = _bf(a_bd)
        t_inv = eye + a_bd
        a_pow = _bf(_dot(a_pow, a_pow))
        yield
        span = 2
        while span < c_:
            t_next = t_inv + _dot(_bf(t_inv), a_pow)
            span *= 2
            if span < c_:
                a_pow = _bf(_dot(a_pow, a_pow))
            t_inv = t_next
            yield

        x1 = _dot(_bf(t_inv), jnp.concatenate([a_st, _bf(akv)], axis=1))
        yield
        a_p = collapse(x1[:, :pw])
        u0 = collapse(x1[:, pw:])
        a_pb, u0b = _bf(a_p), _bf(u0)
        rhs = jnp.concatenate(
            [jnp.concatenate([a_pb, u0b], axis=1),
             jnp.concatenate([jnp.zeros_like(vb), vb], axis=1)], axis=0)
        x2 = _dot(_bf(g_r), rhs)
        m_p = jnp.where(blockdiag, lax.dot_general(a_pb, b_c, _TN, preferred_element_type=F32), 0.0)
        n_n = jnp.where(
            blockdiag,
            lax.dot_general(jnp.concatenate([u0b, vb], axis=0),
                            jnp.concatenate([b_c, k_c], axis=0), _TN,
                            preferred_element_type=F32), 0.0)
        yield
        r_p = r_t + collapse(jnp.where(blockdiag, x2[:, :pw], 0.0))
        y0 = collapse(jnp.where(blockdiag, x2[:, pw:], 0.0))

        state_b = _bf(state)
        o = lax.dot_general(_bf(r_p), state_b, _NT, preferred_element_type=F32) + y0
        new_state = state * w_end + _dot(state_b, _bf(m_p)) + n_n
        yield

        mu_o = segsum(o) * (1.0 / HEAD_DIM)
        yield
        d = o - mu_o
        var = segsum(d * d) * (1.0 / HEAD_DIM)
        yield
        o = d * lax.rsqrt(var + LNX_EPS) * prow(_P_LNG) + prow(_P_LNB)
        return _bf((o + bonus) * jax.nn.silu(xz)), v_shifted, new_state

    def body(c, carry):
        r0 = pl.multiple_of(c * c_, c_)
        cur = sm_ref[pl.ds(r0, c_), :]
        prev_rows = pl.ds(pl.multiple_of(jnp.maximum(r0 - 8, 0), 8), 8)
        last = jnp.where(c > 0, sm_ref[prev_rows, :], carry_sm_scr[...])[7:8]
        prev = jnp.where(first_row_sm, last, pltpu.roll(cur, 1, 0))
        sms = cur + (prev - cur) * mus
        lhs_w = _bf(jnp.where(m_wlo, jnp.tanh(sms), 0.0))
        lhs_a = _bf(jnp.where(m_alo, sms, 0.0))
        lhs_m = _bf(jnp.where(m_mv, sms, 0.0)) if has_mix else None
        pairs = range(wb // pw)
        chains = [pair_chunk(p, load_pair(c, p), lhs_w, lhs_a, lhs_m) for p in pairs]
        results = [None] * len(chains)
        while any(r is None for r in results):
            for p, chain in enumerate(chains):
                try:
                    next(chain)
                except StopIteration as done:
                    results[p] = done.value
        rows = pl.ds(r0, c_)
        for p, (y, v_shifted, new_state) in zip(pairs, results):
            lanes = slice(p * pw, (p + 1) * pw)
            y_ref[rows, lanes] = y
            if not has_mix:
                vout_ref[rows, lanes] = v_shifted
            state_scr[p] = new_state
        return carry

    lax.fori_loop(0, ts // c_, body, 0)

    for i, ref in enumerate(shifted_refs):
        carry_scr[i] = ref[ts - 8:ts, :]
    carry_sm_scr[...] = sm_ref[ts - 8:ts, :]


def _rwkv_call(proj3, small3, v_first, pvec, mu_small, w_lora):
    b, s, _ = proj3.shape
    rw = pvec.shape[1]
    wb, ts = min(RWKV_BLOCK_W, rw), min(RWKV_SEQ_TILE, s)
    groups = rw // wb
    has_mix = v_first is not None
    col0 = 2 * rw // wb

    def col_spec(offset):
        return pl.BlockSpec((None, ts, wb), lambda i, g, j: (i, j, offset + g))

    in_specs = [col_spec(col0), col_spec(col0 + groups), col_spec(col0 + 2 * groups),
                col_spec(col0 + 3 * groups),
                pl.BlockSpec((None, ts, SMALL_W), lambda i, g, j: (i, j, 0))]
    args = [proj3, proj3, proj3, proj3, small3]
    if has_mix:
        in_specs.append(col_spec(0))
        args.append(v_first)
    in_specs += [
        pl.BlockSpec((_P_ROWS, wb), lambda i, g, j: (0, g)),
        pl.BlockSpec((1, SMALL_W), lambda i, g, j: (0, 0)),
        pl.BlockSpec((SMALL_W, wb), lambda i, g, j: (0, g)),
    ]
    args += [pvec, mu_small, w_lora]
    out_specs = [col_spec(0)]
    out_shape = [jax.ShapeDtypeStruct((b, s, rw), BF16)]
    if not has_mix:
        out_specs.append(col_spec(0))
        out_shape.append(jax.ShapeDtypeStruct((b, s, rw), F32))
    res = pl.pallas_call(
        functools.partial(_rwkv_kernel, has_mix),
        grid=(b, groups, s // ts),
        in_specs=in_specs,
        out_specs=out_specs,
        out_shape=out_shape,
        scratch_shapes=[
            pltpu.VMEM((wb // PAIR_W, PAIR_W, PAIR_W), F32),
            pltpu.VMEM((4, 8, wb), F32),
            pltpu.VMEM((8, SMALL_W), F32),
        ],
        compiler_params=_params("parallel", "parallel", "arbitrary"),
        name="rwkv_mixer",
    )(*args)
    return (res[0], v_first) if has_mix else (res[0], res[1])


def _merge_kernel(h_ref, ya_ref, yb_ref, wga_ref, wgb_ref, wa_ref, wb_ref, o_ref):
    h = h_ref[...]
    ga = jax.nn.sigmoid(_dot(h, wga_ref[...]))
    gb = jax.nn.sigmoid(_dot(h, wgb_ref[...]))
    pa = _dot(ya_ref[...], wa_ref[...])
    pb = _dot(yb_ref[...], wb_ref[...])
    o_ref[...] = _bf(ga * pa + gb * pb)


def _merge_call(h, ya, yb, w_gate, w_br_a, w_br_b):
    t, d = h.shape
    ka, kb = ya.shape[1], yb.shape[1]
    tm, tn = min(1024, t), 512
    nj = d // tn
    return pl.pallas_call(
        _merge_kernel,
        grid=(t // tm, nj),
        in_specs=[
            pl.BlockSpec((tm, d), lambda i, j: (i, 0)),
            pl.BlockSpec((tm, ka), lambda i, j: (i, 0)),
            pl.BlockSpec((tm, kb), lambda i, j: (i, 0)),
            pl.BlockSpec((d, tn), lambda i, j: (0, j)),
            pl.BlockSpec((d, tn), lambda i, j: (0, nj + j)),
            pl.BlockSpec((ka, tn), lambda i, j: (0, j)),
            pl.BlockSpec((kb, tn), lambda i, j: (0, j)),
        ],
        out_specs=pl.BlockSpec((tm, tn), lambda i, j: (i, j)),
        out_shape=jax.ShapeDtypeStruct((t, d), BF16),
        compiler_params=_params("parallel", "parallel"),
        name="gated_merge",
    )(h, ya, yb, w_gate, w_gate, w_br_a, w_br_b)


def _out_kernel(m_ref, w_ref, x_ref, gate_ref, g_ref, o_ref):
    out = _dot(m_ref[...], w_ref[...])
    y = out * lax.rsqrt(jnp.mean(out * out, axis=-1, keepdims=True) + RMS_EPS)
    o_ref[...] = x_ref[...] + gate_ref[...] * (y * g_ref[...])


def _out_call(merged, w_out, xf, gate, g_post, seq):
    t, d = xf.shape
    tm = 512
    per_seq = seq // tm
    return pl.pallas_call(
        _out_kernel,
        grid=(t // tm,),
        in_specs=[
            pl.BlockSpec((tm, d), lambda i: (i, 0)),
            pl.BlockSpec((d, d), lambda i: (0, 0)),
            pl.BlockSpec((tm, d), lambda i: (i, 0)),
            pl.BlockSpec((None, 1, d), lambda i: (i // per_seq, 0, 0)),
            pl.BlockSpec((1, d), lambda i: (0, 0)),
        ],
        out_specs=pl.BlockSpec((tm, d), lambda i: (i, 0)),
        out_shape=jax.ShapeDtypeStruct((t, d), F32),
        compiler_params=_params("parallel"),
        name="out_proj",
    )(merged, w_out, xf, gate, g_post)


def kernel(x, c, w_ada, b_ada, g_pre, w_in, w_pool, pool_scale, mu_shift, w_decay_up, w0, w_aaa_up, a0, w_mv_down, mu_mv, w_mv_up, mv0, k_k, k_a, r_k, lnx_g, lnx_b, w_br_a, w_br_b, w_out, g_post):
    bsz, seq, d = x.shape
    depth = w_ada.shape[0]
    pw = pool_scale.shape[1]
    rw = w0.shape[1]
    t = bsz * seq
    n_main = 2 * pw + 4 * rw
    n_lora = DECAY_LORA + AAA_LORA
    n_shift = 4 * rw + n_lora

    xf = x.reshape(t, d)
    c8 = jnp.pad(c, ((0, 8 - bsz), (0, 0)))
    mod = _mod_call(c8, w_ada, b_ada.reshape(depth, 1, 3 * d))

    v_first = None
    for l in range(depth):
        shift = mod[l, :bsz, :d].reshape(bsz, 1, d)
        scale = mod[l, :bsz, d:2 * d].reshape(bsz, 1, d)
        gate = mod[l, :bsz, 2 * d:].reshape(bsz, 1, d)

        w_main = _bf(w_in[l, :, :n_main])
        w_gate = _bf(w_in[l, :, n_main + n_lora:])
        small_cols = [w_in[l, :, n_main:n_main + n_lora]]
        lora_rows = [w_decay_up[l], w_aaa_up[l]]
        mu_small = [mu_shift[l, 4 * rw:n_shift]]
        mv0_l = jnp.zeros((rw,), F32)
        if l > 0:
            small_cols.append(w_mv_down[l - 1])
            lora_rows.append(w_mv_up[l - 1])
            mu_small.append(mu_mv[l - 1])
            mv0_l = mv0[l - 1]
        w_small = jnp.concatenate(small_cols, axis=1)
        w_small = _bf(jnp.pad(w_small, ((0, 0), (0, SMALL_W - w_small.shape[1]))))
        w_lora = jnp.concatenate(lora_rows, axis=0)
        w_lora = _bf(jnp.pad(w_lora, ((0, SMALL_W - w_lora.shape[0]), (0, 0))))
        mu_small = jnp.concatenate(mu_small)
        mu_small = jnp.pad(mu_small, (0, SMALL_W - mu_small.shape[0])).reshape(1, SMALL_W)
        mu4 = mu_shift[l, :4 * rw].reshape(4, rw)
        pvec = jnp.concatenate(
            [mu4, jnp.stack([w0[l], a0[l], mv0_l, k_k[l], k_a[l], r_k[l].reshape(rw),
                             lnx_g[l], lnx_b[l]]),
             jnp.zeros((_P_ROWS - 12, rw), F32)], axis=0)

        h, small = _prenorm_call(xf, g_pre[l].reshape(1, d), scale, shift, w_small, seq)
        proj = _matmul_call(h, w_main, F32, 1024, 1024)
        proj3 = proj.reshape(bsz, seq, n_main)
        small3 = small.reshape(bsz, seq, SMALL_W)

        y_a = _pool_call(proj3, _bf(w_pool[l]), pool_scale[l].reshape(1, pw))
        y_b, v_first = _rwkv_call(proj3, small3, v_first, pvec, mu_small, w_lora)

        merged = _merge_call(h, y_a.reshape(t, pw), y_b.reshape(t, rw), w_gate,
                             _bf(w_br_a[l]), _bf(w_br_b[l]))
        xf = _out_call(merged, _bf(w_out[l]), xf, gate, g_post[l].reshape(1, d), seq)
    return xf.reshape(bsz, seq, d)
```

```python
import functools

import jax
import jax.numpy as jnp
from jax import lax
from jax.experimental import pallas as pl
from jax.experimental.pallas import tpu as pltpu

F32 = jnp.float32
BF16 = jnp.bfloat16

RMS_EPS = 1e-6
LNX_EPS = 64e-5
POOL_WINDOWS = (2, 4, 8, 16)
HEAD_DIM = 64
DECAY_LORA = 64
AAA_LORA = 64
MV_LORA = 32

CHUNK = 64
PAIR_HEADS = 2
PAIR_W = PAIR_HEADS * HEAD_DIM
RWKV_BLOCK_W = 1024
RWKV_SEQ_TILE = 512
RWKV_CHUNKS_PER_STEP = 2
SMALL_W = 256

_MIB = 1024 * 1024
_V7X_VMEM_LIMIT = 56 * _MIB

_EXP_M_HALF = 0.6065306597126334
_NT = (((1,), (1,)), ((), ()))
_TN = (((0,), (0,)), ((), ()))


def _params(*sem):
    return pltpu.CompilerParams(dimension_semantics=sem, vmem_limit_bytes=_V7X_VMEM_LIMIT)


def _dot(a, b):
    return jnp.dot(a, b, preferred_element_type=F32)


def _bf(x):
    return x.astype(BF16)


def _mod_kernel(c_ref, w_ref, b_ref, o_ref):
    cond = _bf(jax.nn.silu(c_ref[...]))
    o_ref[...] = _dot(cond, _bf(w_ref[...])) + b_ref[...]


def _mod_call(c8, w_ada, b_ada3):
    depth, d, n = w_ada.shape
    tn = 512
    return pl.pallas_call(
        _mod_kernel,
        grid=(depth, n // tn),
        in_specs=[
            pl.BlockSpec((8, d), lambda l, j: (0, 0)),
            pl.BlockSpec((None, d, tn), lambda l, j: (l, 0, j)),
            pl.BlockSpec((None, 1, tn), lambda l, j: (l, 0, j)),
        ],
        out_specs=pl.BlockSpec((None, 8, tn), lambda l, j: (l, 0, j)),
        out_shape=jax.ShapeDtypeStruct((depth, 8, n), F32),
        compiler_params=_params("parallel", "parallel"),
        name="adaln_mod",
    )(c8, w_ada, b_ada3)


def _prenorm_kernel(x_ref, g_ref, sc_ref, sh_ref, ws_ref, h_ref, sm_ref):
    x = x_ref[...]
    y = x * lax.rsqrt(jnp.mean(x * x, axis=-1, keepdims=True) + RMS_EPS)
    h = _bf((y * g_ref[...]) * (1.0 + sc_ref[...]) + sh_ref[...])
    h_ref[...] = h
    sm_ref[...] = _dot(h, ws_ref[...])


def _prenorm_call(xf, g, scale, shift, w_small, seq):
    t, d = xf.shape
    tm = 512
    per_seq = seq // tm
    return pl.pallas_call(
        _prenorm_kernel,
        grid=(t // tm,),
        in_specs=[
            pl.BlockSpec((tm, d), lambda i: (i, 0)),
            pl.BlockSpec((1, d), lambda i: (0, 0)),
            pl.BlockSpec((None, 1, d), lambda i: (i // per_seq, 0, 0)),
            pl.BlockSpec((None, 1, d), lambda i: (i // per_seq, 0, 0)),
            pl.BlockSpec((d, SMALL_W), lambda i: (0, 0)),
        ],
        out_specs=[
            pl.BlockSpec((tm, d), lambda i: (i, 0)),
            pl.BlockSpec((tm, SMALL_W), lambda i: (i, 0)),
        ],
        out_shape=[
            jax.ShapeDtypeStruct((t, d), BF16),
            jax.ShapeDtypeStruct((t, SMALL_W), F32),
        ],
        compiler_params=_params("parallel"),
        name="prenorm",
    )(xf, g, scale, shift, w_small)


def _matmul_kernel(a_ref, b_ref, o_ref):
    o_ref[...] = _dot(a_ref[...], b_ref[...]).astype(o_ref.dtype)


def _matmul_call(a, b, out_dtype, tm, tn):
    m, k = a.shape
    n = b.shape[1]
    tm = min(tm, m)
    return pl.pallas_call(
        _matmul_kernel,
        grid=(m // tm, n // tn),
        in_specs=[
            pl.BlockSpec((tm, k), lambda i, j: (i, 0)),
            pl.BlockSpec((k, tn), lambda i, j: (0, j)),
        ],
        out_specs=pl.BlockSpec((tm, tn), lambda i, j: (i, j)),
        out_shape=jax.ShapeDtypeStruct((m, n), out_dtype),
        compiler_params=_params("parallel", "parallel"),
        name="in_proj",
    )(a, b)


def _pool_kernel(u_ref, z_ref, w_ref, ps_ref, y_ref):
    group = pl.program_id(1)
    u = u_ref[...]
    row = lax.broadcasted_iota(jnp.int32, (u.shape[0], 1), 0)

    def shifted(a, s):
        return jnp.where(row >= s, pltpu.roll(a, s, 0), 0.0)

    for gi, window in enumerate(POOL_WINDOWS):

        @pl.when(group == gi)
        def _():
            acc, span = u, 1
            while span < window:
                acc = acc + shifted(acc, span)
                span *= 2
            cnt = jnp.minimum(row + 1, window).astype(F32)
            pooled = acc / cnt - u
            y = _dot(_bf(pooled), w_ref[...])
            y_ref[...] = _bf(y * ps_ref[...] * jax.nn.silu(z_ref[...]))


def _pool_call(proj3, w_pool, pool_scale):
    b, s, _ = proj3.shape
    groups, gd, _ = w_pool.shape
    return pl.pallas_call(
        _pool_kernel,
        grid=(b, groups),
        in_specs=[
            pl.BlockSpec((None, s, gd), lambda i, g: (i, 0, g)),
            pl.BlockSpec((None, s, gd), lambda i, g: (i, 0, groups + g)),
            pl.BlockSpec((None, gd, gd), lambda i, g: (g, 0, 0)),
            pl.BlockSpec((1, gd), lambda i, g: (0, g)),
        ],
        out_specs=pl.BlockSpec((None, s, gd), lambda i, g: (i, 0, g)),
        out_shape=jax.ShapeDtypeStruct((b, s, groups * gd), BF16),
        compiler_params=_params("parallel", "parallel"),
        name="pool_mixer",
    )(proj3, proj3, w_pool, pool_scale)


(_P_MU_R, _P_MU_K, _P_MU_V, _P_MU_Z, _P_W0, _P_A0, _P_MV0, _P_KK, _P_KA, _P_RK,
 _P_LNG, _P_LNB) = range(12)
_P_ROWS = 16


def _split_terms(x, terms):
    pieces, rem = [], x
    for _ in range(terms):
        p = _bf(rem)
        pieces.append(p)
        rem = rem - p.astype(F32)
    return pieces


def _rwkv_kernel(has_mix, *refs):
    if has_mix:
        (r_ref, k_ref, v_ref, z_ref, sm_ref, vf_ref, pv_ref, mus_ref, wl_ref,
         y_ref, state_scr, carry_scr, carry_sm_scr) = refs
    else:
        (r_ref, k_ref, v_ref, z_ref, sm_ref, pv_ref, mus_ref, wl_ref,
         y_ref, vout_ref, state_scr, carry_scr, carry_sm_scr) = refs
    ts, wb = r_ref.shape
    c_, pw, st = CHUNK, PAIR_W, PAIR_HEADS * CHUNK
    shifted_refs = (r_ref, k_ref, v_ref, z_ref)

    @pl.when(pl.program_id(2) == 0)
    def _():
        state_scr[...] = jnp.zeros_like(state_scr)
        carry_scr[...] = jnp.zeros_like(carry_scr)
        carry_sm_scr[...] = jnp.zeros_like(carry_sm_scr)

    pv = pv_ref[...]
    mus = mus_ref[...]

    def iota(shape, dim):
        return lax.broadcasted_iota(jnp.int32, shape, dim)

    first_row = iota((c_, pw), 0) == 0
    first_row_sm = iota((c_, SMALL_W), 0) == 0
    lane_sm = iota((c_, SMALL_W), 1)
    m_wlo = lane_sm < DECAY_LORA
    m_alo = (lane_sm >= DECAY_LORA) & (lane_sm < DECAY_LORA + AAA_LORA)
    m_mv = (lane_sm >= DECAY_LORA + AAA_LORA) & (lane_sm < DECAY_LORA + AAA_LORA + MV_LORA)
    lane = iota((c_, pw), 1)
    head_masks = [lane // HEAD_DIM == j for j in range(PAIR_HEADS)]
    sq_r, sq_c = iota((st, pw), 0), iota((st, pw), 1)
    blockdiag = sq_r // HEAD_DIM == sq_c // HEAD_DIM
    ones_bd = jnp.where(blockdiag, 1.0, 0.0).astype(BF16)
    ones_bd2 = jnp.concatenate([ones_bd, ones_bd], axis=0)
    tok_r, tok_c = iota((c_, 2 * st), 0), iota((c_, 2 * st), 1) % c_
    strict_lower = (tok_r > tok_c)[:, :st]
    lower = tok_r >= tok_c
    eye = jnp.where(tok_r == tok_c, 1.0, 0.0).astype(F32)[:, :st]
    tri = jnp.where(iota((c_, c_), 0) >= iota((c_, c_), 1), 1.0, 0.0).astype(BF16)
    tri3 = jnp.concatenate([tri, tri, tri], axis=1)

    def segsum(x):
        return _dot(jnp.concatenate(_split_terms(x, 2), axis=1), ones_bd2)

    def stack(x):
        return _bf(jnp.concatenate([jnp.where(m, x, 0.0) for m in head_masks], axis=0))

    def blockdiag_of(x):
        return _bf(jnp.where(blockdiag, jnp.concatenate([x] * PAIR_HEADS, axis=0), 0.0))

    def load_pair(c, p):
        lanes = slice(p * pw, (p + 1) * pw)
        r0 = pl.multiple_of(c * c_, c_)
        rows = pl.ds(r0, c_)
        prev_rows = pl.ds(pl.multiple_of(jnp.maximum(r0 - 8, 0), 8), 8)
        cur = [ref[rows, lanes] for ref in shifted_refs]
        last = [jnp.where(c > 0, ref[prev_rows, lanes], carry_scr[i, :, lanes])[7:8]
                for i, ref in enumerate(shifted_refs)]
        vf = vf_ref[rows, lanes] if has_mix else None
        return cur, last, vf

    def pair_chunk(p, loaded, lw, la, lm, states, state_delay):
        lanes = slice(p * pw, (p + 1) * pw)
        cur, last, vf = loaded
        prow = lambda i: pv[i:i + 1, lanes]

        def tshift(i, mu):
            prev = jnp.where(first_row, last[i], pltpu.roll(cur[i], 1, 0))
            return cur[i] + (prev - cur[i]) * mu

        xr = tshift(0, prow(_P_MU_R))
        xk = tshift(1, prow(_P_MU_K))
        xv = tshift(2, prow(_P_MU_V))
        xz = tshift(3, prow(_P_MU_Z))
        v_shifted = xv

        ld = -_EXP_M_HALF * jax.nn.sigmoid(prow(_P_W0) + lw)
        asig = jax.nn.sigmoid(prow(_P_A0) + la)
        if has_mix:
            xv = xv + (vf - xv) * jax.nn.sigmoid(prow(_P_MV0) + lm)
        kk = xk * prow(_P_KK)
        kp = xk * (1.0 + (asig - 1.0) * prow(_P_KA))
        kk_sq = segsum(kk * kk)
        rk_sum = segsum(xr * kp * prow(_P_RK))
        cum = _dot(tri3, jnp.concatenate(_split_terms(ld, 3), axis=0))
        yield
        kkn = kk * lax.rsqrt(jnp.maximum(kk_sq, 1e-24))
        a = -kkn
        b = kkn * asig
        bonus = rk_sum * xv
        e_in = jnp.exp(cum)
        e_out = jnp.exp(-cum)
        w_end = e_in[c_ - 1:c_, :]
        e_end = w_end * e_out
        a_t = a * jnp.exp(cum - ld)
        r_t = xr * e_in
        b_t = b * e_out
        k_t = kp * e_out
        b_c = _bf(b * e_end)
        k_c = _bf(kp * e_end)
        v_st = stack(xv)

        g = lax.dot_general(_bf(jnp.concatenate([a_t, r_t], axis=0)),
                            jnp.concatenate([stack(b_t), stack(k_t)], axis=0), _NT,
                            preferred_element_type=F32)
        a_ab = jnp.where(strict_lower, g[:c_, :st], 0.0)
        a_ak = jnp.where(strict_lower, g[:c_, st:], 0.0)
        a_r = _bf(jnp.where(lower, g[c_:, :], 0.0))
        yield

        akv = _dot(_bf(a_ak), v_st)
        t_inv = eye + a_ab
        a_pow = _dot(_bf(a_ab), blockdiag_of(a_ab))
        yield
        span = 2
        while span < c_:
            pow_bd = blockdiag_of(a_pow)
            span *= 2
            if span < c_:
                both = _dot(_bf(jnp.concatenate([t_inv, a_pow], axis=0)), pow_bd)
                t_inv, a_pow = t_inv + both[:c_], both[c_:]
            else:
                t_inv = t_inv + _dot(_bf(t_inv), pow_bd)
            yield

        x1 = _dot(_bf(t_inv), jnp.concatenate([stack(a_t), stack(akv)], axis=1))
        yield
        a_p, u0 = x1[:, :pw], x1[:, pw:]
        rhs = jnp.concatenate(
            [jnp.concatenate([stack(a_p), stack(u0)], axis=1),
             jnp.concatenate([jnp.zeros_like(v_st), v_st], axis=1)], axis=0)
        x2 = _dot(a_r, rhs)
        yield
        r_p = r_t + x2[:, :pw]
        y0 = x2[:, pw:]
        for _ in range(state_delay):
            yield

        state = states[p]
        ou = lax.dot_general(_bf(jnp.concatenate([r_p, a_p], axis=0)), _bf(state), _NT,
                             preferred_element_type=F32)
        yield
        o = ou[:c_] + y0
        u = ou[c_:] + u0
        states[p] = state * w_end + jnp.where(
            blockdiag,
            lax.dot_general(_bf(jnp.concatenate([u, xv], axis=0)),
                            jnp.concatenate([b_c, k_c], axis=0), _TN,
                            preferred_element_type=F32), 0.0)

        mu_o = segsum(o) * (1.0 / HEAD_DIM)
        yield
        d = o - mu_o
        var = segsum(d * d) * (1.0 / HEAD_DIM)
        yield
        o = d * lax.rsqrt(var + LNX_EPS) * prow(_P_LNG) + prow(_P_LNB)
        return _bf((o + bonus) * jax.nn.silu(xz)), v_shifted

    def lora_up(c):
        r0 = pl.multiple_of(c * c_, c_)
        cur = sm_ref[pl.ds(r0, c_), :]
        prev_rows = pl.ds(pl.multiple_of(jnp.maximum(r0 - 8, 0), 8), 8)
        last = jnp.where(c > 0, sm_ref[prev_rows, :], carry_sm_scr[...])[7:8]
        prev = jnp.where(first_row_sm, last, pltpu.roll(cur, 1, 0))
        sms = cur + (prev - cur) * mus
        wl = wl_ref[...]
        lw = _dot(_bf(jnp.where(m_wlo, jnp.tanh(sms), 0.0)), wl)
        la = _dot(_bf(jnp.where(m_alo, sms, 0.0)), wl)
        lm = _dot(_bf(jnp.where(m_mv, sms, 0.0)), wl) if has_mix else None
        return lw, la, lm

    pairs = range(wb // pw)
    lane_blk = lambda x, p: None if x is None else x[:, p * pw:(p + 1) * pw]

    def body(step, carry):
        states = [state_scr[p] for p in pairs]
        chunks = [step * RWKV_CHUNKS_PER_STEP + j for j in range(RWKV_CHUNKS_PER_STEP)]
        chains = []
        for j, c in enumerate(chunks):
            lw, la, lm = lora_up(c)
            chains += [pair_chunk(p, load_pair(c, p), lane_blk(lw, p), lane_blk(la, p),
                                  lane_blk(lm, p), states, 2 * j) for p in pairs]
        results = [None] * len(chains)
        while any(r is None for r in results):
            for i, chain in enumerate(chains):
                if results[i] is not None:
                    continue
                try:
                    next(chain)
                except StopIteration as done:
                    results[i] = done.value
        for i, (y, v_shifted) in enumerate(results):
            c, p = chunks[i // len(pairs)], i % len(pairs)
            rows = pl.ds(pl.multiple_of(c * c_, c_), c_)
            lanes = slice(p * pw, (p + 1) * pw)
            y_ref[rows, lanes] = y
            if not has_mix:
                vout_ref[rows, lanes] = v_shifted
        for p in pairs:
            state_scr[p] = states[p]
        return carry

    lax.fori_loop(0, ts // (c_ * RWKV_CHUNKS_PER_STEP), body, 0)

    for i, ref in enumerate(shifted_refs):
        carry_scr[i] = ref[ts - 8:ts, :]
    carry_sm_scr[...] = sm_ref[ts - 8:ts, :]


def _rwkv_call(proj3, small3, v_first, pvec, mu_small, w_lora):
    b, s, _ = proj3.shape
    rw = pvec.shape[1]
    wb, ts = min(RWKV_BLOCK_W, rw), min(RWKV_SEQ_TILE, s)
    groups = rw // wb
    has_mix = v_first is not None
    col0 = 2 * rw // wb

    def col_spec(offset):
        return pl.BlockSpec((None, ts, wb), lambda i, g, j: (i, j, offset + g))

    in_specs = [col_spec(col0), col_spec(col0 + groups), col_spec(col0 + 2 * groups),
                col_spec(col0 + 3 * groups),
                pl.BlockSpec((None, ts, SMALL_W), lambda i, g, j: (i, j, 0))]
    args = [proj3, proj3, proj3, proj3, small3]
    if has_mix:
        in_specs.append(col_spec(0))
        args.append(v_first)
    in_specs += [
        pl.BlockSpec((_P_ROWS, wb), lambda i, g, j: (0, g)),
        pl.BlockSpec((1, SMALL_W), lambda i, g, j: (0, 0)),
        pl.BlockSpec((SMALL_W, wb), lambda i, g, j: (0, g)),
    ]
    args += [pvec, mu_small, w_lora]
    out_specs = [col_spec(0)]
    out_shape = [jax.ShapeDtypeStruct((b, s, rw), BF16)]
    if not has_mix:
        out_specs.append(col_spec(0))
        out_shape.append(jax.ShapeDtypeStruct((b, s, rw), F32))
    res = pl.pallas_call(
        functools.partial(_rwkv_kernel, has_mix),
        grid=(b, groups, s // ts),
        in_specs=in_specs,
        out_specs=out_specs,
        out_shape=out_shape,
        scratch_shapes=[
            pltpu.VMEM((wb // PAIR_W, PAIR_W, PAIR_W), F32),
            pltpu.VMEM((4, 8, wb), F32),
            pltpu.VMEM((8, SMALL_W), F32),
        ],
        compiler_params=_params("parallel", "parallel", "arbitrary"),
        name="rwkv_mixer",
    )(*args)
    return (res[0], v_first) if has_mix else (res[0], res[1])


def _merge_kernel(h_ref, ya_ref, yb_ref, wga_ref, wgb_ref, wa_ref, wb_ref, o_ref):
    h = h_ref[...]
    ga = jax.nn.sigmoid(_dot(h, wga_ref[...]))
    gb = jax.nn.sigmoid(_dot(h, wgb_ref[...]))
    pa = _dot(ya_ref[...], wa_ref[...])
    pb = _dot(yb_ref[...], wb_ref[...])
    o_ref[...] = _bf(ga * pa + gb * pb)


def _merge_call(h, ya, yb, w_gate, w_br_a, w_br_b):
    t, d = h.shape
    ka, kb = ya.shape[1], yb.shape[1]
    tm, tn = min(1024, t), 512
    nj = d // tn
    return pl.pallas_call(
        _merge_kernel,
        grid=(t // tm, nj),
        in_specs=[
            pl.BlockSpec((tm, d), lambda i, j: (i, 0)),
            pl.BlockSpec((tm, ka), lambda i, j: (i, 0)),
            pl.BlockSpec((tm, kb), lambda i, j: (i, 0)),
            pl.BlockSpec((d, tn), lambda i, j: (0, j)),
            pl.BlockSpec((d, tn), lambda i, j: (0, nj + j)),
            pl.BlockSpec((ka, tn), lambda i, j: (0, j)),
            pl.BlockSpec((kb, tn), lambda i, j: (0, j)),
        ],
        out_specs=pl.BlockSpec((tm, tn), lambda i, j: (i, j)),
        out_shape=jax.ShapeDtypeStruct((t, d), BF16),
        compiler_params=_params("parallel", "parallel"),
        name="gated_merge",
    )(h, ya, yb, w_gate, w_gate, w_br_a, w_br_b)


def _out_kernel(m_ref, w_ref, x_ref, gate_ref, g_ref, o_ref):
    out = _dot(m_ref[...], w_ref[...])
    y = out * lax.rsqrt(jnp.mean(out * out, axis=-1, keepdims=True) + RMS_EPS)
    o_ref[...] = x_ref[...] + gate_ref[...] * (y * g_ref[...])


def _out_call(merged, w_out, xf, gate, g_post, seq):
    t, d = xf.shape
    tm = 512
    per_seq = seq // tm
    return pl.pallas_call(
        _out_kernel,
        grid=(t // tm,),
        in_specs=[
            pl.BlockSpec((tm, d), lambda i: (i, 0)),
            pl.BlockSpec((d, d), lambda i: (0, 0)),
            pl.BlockSpec((tm, d), lambda i: (i, 0)),
            pl.BlockSpec((None, 1, d), lambda i: (i // per_seq, 0, 0)),
            pl.BlockSpec((1, d), lambda i: (0, 0)),
        ],
        out_specs=pl.BlockSpec((tm, d), lambda i: (i, 0)),
        out_shape=jax.ShapeDtypeStruct((t, d), F32),
        compiler_params=_params("parallel"),
        name="out_proj",
    )(merged, w_out, xf, gate, g_post)


def kernel(x, c, w_ada, b_ada, g_pre, w_in, w_pool, pool_scale, mu_shift, w_decay_up, w0, w_aaa_up, a0, w_mv_down, mu_mv, w_mv_up, mv0, k_k, k_a, r_k, lnx_g, lnx_b, w_br_a, w_br_b, w_out, g_post):
    bsz, seq, d = x.shape
    depth = w_ada.shape[0]
    pw = pool_scale.shape[1]
    rw = w0.shape[1]
    t = bsz * seq
    n_main = 2 * pw + 4 * rw
    n_lora = DECAY_LORA + AAA_LORA
    n_shift = 4 * rw + n_lora

    xf = x.reshape(t, d)
    c8 = jnp.pad(c, ((0, 8 - bsz), (0, 0)))
    mod = _mod_call(c8, w_ada, b_ada.reshape(depth, 1, 3 * d))

    v_first = None
    for l in range(depth):
        shift = mod[l, :bsz, :d].reshape(bsz, 1, d)
        scale = mod[l, :bsz, d:2 * d].reshape(bsz, 1, d)
        gate = mod[l, :bsz, 2 * d:].reshape(bsz, 1, d)

        w_main = _bf(w_in[l, :, :n_main])
        w_gate = _bf(w_in[l, :, n_main + n_lora:])
        small_cols = [w_in[l, :, n_main:n_main + n_lora]]
        lora_rows = [w_decay_up[l], w_aaa_up[l]]
        mu_small = [mu_shift[l, 4 * rw:n_shift]]
        mv0_l = jnp.zeros((rw,), F32)
        if l > 0:
            small_cols.append(w_mv_down[l - 1])
            lora_rows.append(w_mv_up[l - 1])
            mu_small.append(mu_mv[l - 1])
            mv0_l = mv0[l - 1]
        w_small = jnp.concatenate(small_cols, axis=1)
        w_small = _bf(jnp.pad(w_small, ((0, 0), (0, SMALL_W - w_small.shape[1]))))
        w_lora = jnp.concatenate(lora_rows, axis=0)
        w_lora = _bf(jnp.pad(w_lora, ((0, SMALL_W - w_lora.shape[0]), (0, 0))))
        mu_small = jnp.concatenate(mu_small)
        mu_small = jnp.pad(mu_small, (0, SMALL_W - mu_small.shape[0])).reshape(1, SMALL_W)
        mu4 = mu_shift[l, :4 * rw].reshape(4, rw)
        pvec = jnp.concatenate(
            [mu4, jnp.stack([w0[l], a0[l], mv0_l, k_k[l], k_a[l], r_k[l].reshape(rw),
                             lnx_g[l], lnx_b[l]]),
             jnp.zeros((_P_ROWS - 12, rw), F32)], axis=0)

        h, small = _prenorm_call(xf, g_pre[l].reshape(1, d), scale, shift, w_small, seq)
        proj = _matmul_call(h, w_main, F32, 1024, 1024)
        proj3 = proj.reshape(bsz, seq, n_main)
        small3 = small.reshape(bsz, seq, SMALL_W)

        y_a = _pool_call(proj3, _bf(w_pool[l]), pool_scale[l].reshape(1, pw))
        y_b, v_first = _rwkv_call(proj3, small3, v_first, pvec, mu_small, w_lora)

        merged = _merge_call(h, y_a.reshape(t, pw), y_b.reshape(t, rw), w_gate,
                             _bf(w_br_a[l]), _bf(w_br_b[l]))
        xf = _out_call(merged, _bf(w_out[l]), xf, gate, g_post[l].reshape(1, d), seq)
    return xf.reshape(bsz, seq, d)
```

```python
import functools

import jax
import jax.numpy as jnp
from jax import lax
from jax.experimental import pallas as pl
from jax.experimental.pallas import tpu as pltpu

F32 = jnp.float32
BF16 = jnp.bfloat16

RMS_EPS = 1e-6
LNX_EPS = 64e-5
POOL_WINDOWS = (2, 4, 8, 16)
HEAD_DIM = 64
DECAY_LORA = 64
AAA_LORA = 64
MV_LORA = 32

CHUNK = 64
PAIR_HEADS = 2
PAIR_W = PAIR_HEADS * HEAD_DIM
RWKV_BLOCK_W = 1024
RWKV_SEQ_TILE = 512
RWKV_CHUNKS_PER_STEP = 4
SMALL_W = 256

_MIB = 1024 * 1024
_V7X_VMEM_LIMIT = 56 * _MIB

_EXP_M_HALF = 0.6065306597126334
_NT = (((1,), (1,)), ((), ()))
_TN = (((0,), (0,)), ((), ()))


def _params(*sem):
    return pltpu.CompilerParams(dimension_semantics=sem, vmem_limit_bytes=_V7X_VMEM_LIMIT)


def _dot(a, b):
    return jnp.dot(a, b, preferred_element_type=F32)


def _bf(x):
    return x.astype(BF16)


def _mod_kernel(c_ref, w_ref, b_ref, o_ref):
    cond = _bf(jax.nn.silu(c_ref[...]))
    o_ref[...] = _dot(cond, _bf(w_ref[...])) + b_ref[...]


def _mod_call(c8, w_ada, b_ada3):
    depth, d, n = w_ada.shape
    tn = 512
    return pl.pallas_call(
        _mod_kernel,
        grid=(depth, n // tn),
        in_specs=[
            pl.BlockSpec((8, d), lambda l, j: (0, 0)),
            pl.BlockSpec((None, d, tn), lambda l, j: (l, 0, j)),
            pl.BlockSpec((None, 1, tn), lambda l, j: (l, 0, j)),
        ],
        out_specs=pl.BlockSpec((None, 8, tn), lambda l, j: (l, 0, j)),
        out_shape=jax.ShapeDtypeStruct((depth, 8, n), F32),
        compiler_params=_params("parallel", "parallel"),
        name="adaln_mod",
    )(c8, w_ada, b_ada3)


def _prenorm_kernel(x_ref, g_ref, sc_ref, sh_ref, ws_ref, h_ref, sm_ref):
    x = x_ref[...]
    y = x * lax.rsqrt(jnp.mean(x * x, axis=-1, keepdims=True) + RMS_EPS)
    h = _bf((y * g_ref[...]) * (1.0 + sc_ref[...]) + sh_ref[...])
    h_ref[...] = h
    sm_ref[...] = _dot(h, ws_ref[...])


def _prenorm_call(xf, g, scale, shift, w_small, seq):
    t, d = xf.shape
    tm = 512
    per_seq = seq // tm
    return pl.pallas_call(
        _prenorm_kernel,
        grid=(t // tm,),
        in_specs=[
            pl.BlockSpec((tm, d), lambda i: (i, 0)),
            pl.BlockSpec((1, d), lambda i: (0, 0)),
            pl.BlockSpec((None, 1, d), lambda i: (i // per_seq, 0, 0)),
            pl.BlockSpec((None, 1, d), lambda i: (i // per_seq, 0, 0)),
            pl.BlockSpec((d, SMALL_W), lambda i: (0, 0)),
        ],
        out_specs=[
            pl.BlockSpec((tm, d), lambda i: (i, 0)),
            pl.BlockSpec((tm, SMALL_W), lambda i: (i, 0)),
        ],
        out_shape=[
            jax.ShapeDtypeStruct((t, d), BF16),
            jax.ShapeDtypeStruct((t, SMALL_W), F32),
        ],
        compiler_params=_params("parallel"),
        name="prenorm",
    )(xf, g, scale, shift, w_small)


def _in_proj_kernel(h_ref, w_ref, o_ref, w_scr):
    @pl.when(pl.program_id(1) == 0)
    def _():
        w_scr[...] = _bf(w_ref[...])

    o_ref[...] = _dot(h_ref[...], w_scr[...])


def _in_proj_call(h, w_in, layer, n_cols):
    m, k = h.shape
    tm, tn = min(1024, m), 1024
    return pl.pallas_call(
        _in_proj_kernel,
        grid=(n_cols // tn, m // tm),
        in_specs=[
            pl.BlockSpec((tm, k), lambda j, i: (i, 0)),
            pl.BlockSpec((None, k, tn), lambda j, i: (layer, 0, j)),
        ],
        out_specs=pl.BlockSpec((tm, tn), lambda j, i: (i, j)),
        out_shape=jax.ShapeDtypeStruct((m, n_cols), F32),
        scratch_shapes=[pltpu.VMEM((k, tn), BF16)],
        compiler_params=_params("parallel", "arbitrary"),
        name="in_proj",
    )(h, w_in)


def _pool_kernel(u_ref, z_ref, w_ref, ps_ref, y_ref):
    group = pl.program_id(1)
    u = u_ref[...]
    row = lax.broadcasted_iota(jnp.int32, (u.shape[0], 1), 0)

    def shifted(a, s):
        return jnp.where(row >= s, pltpu.roll(a, s, 0), 0.0)

    for gi, window in enumerate(POOL_WINDOWS):

        @pl.when(group == gi)
        def _():
            acc, span = u, 1
            while span < window:
                acc = acc + shifted(acc, span)
                span *= 2
            cnt = jnp.minimum(row + 1, window).astype(F32)
            pooled = acc / cnt - u
            y = _dot(_bf(pooled), w_ref[...])
            y_ref[...] = _bf(y * ps_ref[...] * jax.nn.silu(z_ref[...]))


def _pool_call(proj3, w_pool, pool_scale):
    b, s, _ = proj3.shape
    groups, gd, _ = w_pool.shape
    return pl.pallas_call(
        _pool_kernel,
        grid=(b, groups),
        in_specs=[
            pl.BlockSpec((None, s, gd), lambda i, g: (i, 0, g)),
            pl.BlockSpec((None, s, gd), lambda i, g: (i, 0, groups + g)),
            pl.BlockSpec((None, gd, gd), lambda i, g: (g, 0, 0)),
            pl.BlockSpec((1, gd), lambda i, g: (0, g)),
        ],
        out_specs=pl.BlockSpec((None, s, gd), lambda i, g: (i, 0, g)),
        out_shape=jax.ShapeDtypeStruct((b, s, groups * gd), BF16),
        compiler_params=_params("parallel", "parallel"),
        name="pool_mixer",
    )(proj3, proj3, w_pool, pool_scale)


(_P_MU_R, _P_MU_K, _P_MU_V, _P_MU_Z, _P_W0, _P_A0, _P_MV0, _P_KK, _P_KA, _P_RK,
 _P_LNG, _P_LNB) = range(12)
_P_ROWS = 16


def _split_terms(x, terms):
    pieces, rem = [], x
    for _ in range(terms):
        p = _bf(rem)
        pieces.append(p)
        rem = rem - p.astype(F32)
    return pieces


def _rwkv_kernel(has_mix, *refs):
    if has_mix:
        (r_ref, k_ref, v_ref, z_ref, sm_ref, vf_ref, pv_ref, mus_ref, wl_ref,
         y_ref, state_scr, carry_scr, carry_sm_scr) = refs
    else:
        (r_ref, k_ref, v_ref, z_ref, sm_ref, pv_ref, mus_ref, wl_ref,
         y_ref, vout_ref, state_scr, carry_scr, carry_sm_scr) = refs
    ts, wb = r_ref.shape
    c_, pw, st = CHUNK, PAIR_W, PAIR_HEADS * CHUNK
    shifted_refs = (r_ref, k_ref, v_ref, z_ref)

    @pl.when(pl.program_id(2) == 0)
    def _():
        state_scr[...] = jnp.zeros_like(state_scr)
        carry_scr[...] = jnp.zeros_like(carry_scr)
        carry_sm_scr[...] = jnp.zeros_like(carry_sm_scr)

    pv = pv_ref[...]
    mus = mus_ref[...]

    def iota(shape, dim):
        return lax.broadcasted_iota(jnp.int32, shape, dim)

    first_row = iota((c_, pw), 0) == 0
    first_row_sm = iota((c_, SMALL_W), 0) == 0
    lane_sm = iota((c_, SMALL_W), 1)
    m_wlo = lane_sm < DECAY_LORA
    m_alo = (lane_sm >= DECAY_LORA) & (lane_sm < DECAY_LORA + AAA_LORA)
    m_mv = (lane_sm >= DECAY_LORA + AAA_LORA) & (lane_sm < DECAY_LORA + AAA_LORA + MV_LORA)
    lane = iota((c_, pw), 1)
    head_masks = [lane // HEAD_DIM == j for j in range(PAIR_HEADS)]
    sq_r, sq_c = iota((st, pw), 0), iota((st, pw), 1)
    blockdiag = sq_r // HEAD_DIM == sq_c // HEAD_DIM
    ones_bd = jnp.where(blockdiag, 1.0, 0.0).astype(BF16)
    ones_bd2 = jnp.concatenate([ones_bd, ones_bd], axis=0)
    tok_r, tok_c = iota((c_, 2 * st), 0), iota((c_, 2 * st), 1) % c_
    strict_lower = (tok_r > tok_c)[:, :st]
    lower = tok_r >= tok_c
    eye = jnp.where(tok_r == tok_c, 1.0, 0.0).astype(F32)[:, :st]
    tri = jnp.where(iota((c_, c_), 0) >= iota((c_, c_), 1), 1.0, 0.0).astype(BF16)
    tri2 = jnp.concatenate([tri, tri], axis=1)

    def segsum(x):
        return _dot(jnp.concatenate(_split_terms(x, 2), axis=1), ones_bd2)

    def stack(x):
        return _bf(jnp.concatenate([jnp.where(m, x, 0.0) for m in head_masks], axis=0))

    def blockdiag_of(x):
        return _bf(jnp.where(blockdiag, jnp.concatenate([x] * PAIR_HEADS, axis=0), 0.0))

    def load_pair(c, p):
        lanes = slice(p * pw, (p + 1) * pw)
        r0 = pl.multiple_of(c * c_, c_)
        rows = pl.ds(r0, c_)
        prev_rows = pl.ds(pl.multiple_of(jnp.maximum(r0 - 8, 0), 8), 8)
        cur = [ref[rows, lanes] for ref in shifted_refs]
        last = [jnp.where(c > 0, ref[prev_rows, lanes], carry_scr[i, :, lanes])[7:8]
                for i, ref in enumerate(shifted_refs)]
        vf = vf_ref[rows, lanes] if has_mix else None
        return cur, last, vf

    def pair_chunk(p, loaded, lw, la, lm, states, state_delay):
        lanes = slice(p * pw, (p + 1) * pw)
        cur, last, vf = loaded
        prow = lambda i: pv[i:i + 1, lanes]

        def tshift(i, mu):
            prev = jnp.where(first_row, last[i], pltpu.roll(cur[i], 1, 0))
            return cur[i] + (prev - cur[i]) * mu

        xr = tshift(0, prow(_P_MU_R))
        xk = tshift(1, prow(_P_MU_K))
        xv = tshift(2, prow(_P_MU_V))
        xz = tshift(3, prow(_P_MU_Z))
        v_shifted = xv

        ld = -_EXP_M_HALF * jax.nn.sigmoid(prow(_P_W0) + lw)
        asig = jax.nn.sigmoid(prow(_P_A0) + la)
        if has_mix:
            xv = xv + (vf - xv) * jax.nn.sigmoid(prow(_P_MV0) + lm)
        kk = xk * prow(_P_KK)
        kp = xk * (1.0 + (asig - 1.0) * prow(_P_KA))
        cum = _dot(tri2, jnp.concatenate(_split_terms(ld, 2), axis=0))
        kk_sq, rk_sum = yield [kk * kk, xr * kp * prow(_P_RK)]
        kkn = kk * lax.rsqrt(jnp.maximum(kk_sq, 1e-24))
        a = -kkn
        b = kkn * asig
        bonus = rk_sum * xv
        e_in = jnp.exp(cum)
        e_out = jnp.exp(-cum)
        w_end = e_in[c_ - 1:c_, :]
        e_end = w_end * e_out
        a_t = a * jnp.exp(cum - ld)
        r_t = xr * e_in
        b_t = b * e_out
        k_t = kp * e_out
        b_c = _bf(b * e_end)
        k_c = _bf(kp * e_end)
        v_st = stack(xv)

        g = lax.dot_general(_bf(jnp.concatenate([a_t, r_t], axis=0)),
                            jnp.concatenate([stack(b_t), stack(k_t)], axis=0), _NT,
                            preferred_element_type=F32)
        a_ab = jnp.where(strict_lower, g[:c_, :st], 0.0)
        a_ak = jnp.where(strict_lower, g[:c_, st:], 0.0)
        a_r = _bf(jnp.where(lower, g[c_:, :], 0.0))
        yield

        akv = _dot(_bf(a_ak), v_st)
        t_inv = eye + a_ab
        a_pow = _dot(_bf(a_ab), blockdiag_of(a_ab))
        yield
        span = 2
        while span < c_:
            pow_bd = blockdiag_of(a_pow)
            span *= 2
            if span < c_:
                both = _dot(_bf(jnp.concatenate([t_inv, a_pow], axis=0)), pow_bd)
                t_inv, a_pow = t_inv + both[:c_], both[c_:]
            else:
                t_inv = t_inv + _dot(_bf(t_inv), pow_bd)
            yield

        x1 = _dot(_bf(t_inv), jnp.concatenate([stack(a_t), stack(akv)], axis=1))
        yield
        a_p, u0 = x1[:, :pw], x1[:, pw:]
        x2 = _dot(a_r[:, :st], jnp.concatenate([stack(a_p), stack(u0)], axis=1))
        x3 = _dot(a_r[:, st:], v_st)
        yield
        r_p = r_t + x2[:, :pw]
        y0 = x2[:, pw:] + x3
        for _ in range(state_delay):
            yield

        state = states[p]
        ou = lax.dot_general(_bf(jnp.concatenate([r_p, a_p], axis=0)), _bf(state), _NT,
                             preferred_element_type=F32)
        yield
        o = ou[:c_] + y0
        u = ou[c_:] + u0
        states[p] = state * w_end + jnp.where(
            blockdiag,
            lax.dot_general(_bf(jnp.concatenate([u, xv], axis=0)),
                            jnp.concatenate([b_c, k_c], axis=0), _TN,
                            preferred_element_type=F32), 0.0)

        (sum_o,) = yield [o]
        d = o - sum_o * (1.0 / HEAD_DIM)
        (sum_dd,) = yield [d * d]
        var = sum_dd * (1.0 / HEAD_DIM)
        o = d * lax.rsqrt(var + LNX_EPS) * prow(_P_LNG) + prow(_P_LNB)
        return _bf((o + bonus) * jax.nn.silu(xz)), v_shifted

    def lora_up(c):
        r0 = pl.multiple_of(c * c_, c_)
        cur = sm_ref[pl.ds(r0, c_), :]
        prev_rows = pl.ds(pl.multiple_of(jnp.maximum(r0 - 8, 0), 8), 8)
        last = jnp.where(c > 0, sm_ref[prev_rows, :], carry_sm_scr[...])[7:8]
        prev = jnp.where(first_row_sm, last, pltpu.roll(cur, 1, 0))
        sms = cur + (prev - cur) * mus
        wl = wl_ref[...]
        lw = _dot(_bf(jnp.where(m_wlo, jnp.tanh(sms), 0.0)), wl)
        la = _dot(_bf(jnp.where(m_alo, sms, 0.0)), wl)
        lm = _dot(_bf(jnp.where(m_mv, sms, 0.0)), wl) if has_mix else None
        return lw, la, lm

    pairs = range(wb // pw)
    lane_blk = lambda x, p: None if x is None else x[:, p * pw:(p + 1) * pw]

    def body(step, carry):
        states = [state_scr[p] for p in pairs]
        chunks = [step * RWKV_CHUNKS_PER_STEP + j for j in range(RWKV_CHUNKS_PER_STEP)]
        chains = []
        for j, c in enumerate(chunks):
            lw, la, lm = lora_up(c)
            chains += [pair_chunk(p, load_pair(c, p), lane_blk(lw, p), lane_blk(la, p),
                                  lane_blk(lm, p), states, 2 * j) for p in pairs]
        results = [None] * len(chains)
        replies = [None] * len(chains)
        while any(r is None for r in results):
            requests = []
            for i, chain in enumerate(chains):
                if results[i] is not None:
                    continue
                try:
                    wanted = chain.send(replies[i])
                except StopIteration as done:
                    results[i] = done.value
                    continue
                replies[i] = None
                if wanted is not None:
                    requests.append((i, wanted))
            if requests:
                sums = segsum(jnp.concatenate([x for _, xs in requests for x in xs], axis=0))
                row = 0
                for i, xs in requests:
                    replies[i] = [sums[row + k * c_:row + (k + 1) * c_] for k in range(len(xs))]
                    row += len(xs) * c_
        for i, (y, v_shifted) in enumerate(results):
            c, p = chunks[i // len(pairs)], i % len(pairs)
            rows = pl.ds(pl.multiple_of(c * c_, c_), c_)
            lanes = slice(p * pw, (p + 1) * pw)
            y_ref[rows, lanes] = y
            if not has_mix:
                vout_ref[rows, lanes] = v_shifted
        for p in pairs:
            state_scr[p] = states[p]
        return carry

    lax.fori_loop(0, ts // (c_ * RWKV_CHUNKS_PER_STEP), body, 0)

    for i, ref in enumerate(shifted_refs):
        carry_scr[i] = ref[ts - 8:ts, :]
    carry_sm_scr[...] = sm_ref[ts - 8:ts, :]


def _rwkv_call(proj3, small3, v_first, pvec, mu_small, w_lora):
    b, s, _ = proj3.shape
    rw = pvec.shape[1]
    wb, ts = min(RWKV_BLOCK_W, rw), min(RWKV_SEQ_TILE, s)
    groups = rw // wb
    has_mix = v_first is not None
    col0 = 2 * rw // wb

    def col_spec(offset):
        return pl.BlockSpec((None, ts, wb), lambda i, g, j: (i, j, offset + g))

    in_specs = [col_spec(col0), col_spec(col0 + groups), col_spec(col0 + 2 * groups),
                col_spec(col0 + 3 * groups),
                pl.BlockSpec((None, ts, SMALL_W), lambda i, g, j: (i, j, 0))]
    args = [proj3, proj3, proj3, proj3, small3]
    if has_mix:
        in_specs.append(col_spec(0))
        args.append(v_first)
    in_specs += [
        pl.BlockSpec((_P_ROWS, wb), lambda i, g, j: (0, g)),
        pl.BlockSpec((1, SMALL_W), lambda i, g, j: (0, 0)),
        pl.BlockSpec((SMALL_W, wb), lambda i, g, j: (0, g)),
    ]
    args += [pvec, mu_small, w_lora]
    out_specs = [col_spec(0)]
    out_shape = [jax.ShapeDtypeStruct((b, s, rw), BF16)]
    if not has_mix:
        out_specs.append(col_spec(0))
        out_shape.append(jax.ShapeDtypeStruct((b, s, rw), F32))
    res = pl.pallas_call(
        functools.partial(_rwkv_kernel, has_mix),
        grid=(b, groups, s // ts),
        in_specs=in_specs,
        out_specs=out_specs,
        out_shape=out_shape,
        scratch_shapes=[
            pltpu.VMEM((wb // PAIR_W, PAIR_W, PAIR_W), F32),
            pltpu.VMEM((4, 8, wb), F32),
            pltpu.VMEM((8, SMALL_W), F32),
        ],
        compiler_params=_params("parallel", "parallel", "arbitrary"),
        name="rwkv_mixer",
    )(*args)
    return (res[0], v_first) if has_mix else (res[0], res[1])


def _merge_kernel(h_ref, ya_ref, yb_ref, wga_ref, wgb_ref, wa_ref, wb_ref, o_ref):
    h = h_ref[...]
    ga = jax.nn.sigmoid(_dot(h, wga_ref[...]))
    gb = jax.nn.sigmoid(_dot(h, wgb_ref[...]))
    pa = _dot(ya_ref[...], wa_ref[...])
    pb = _dot(yb_ref[...], wb_ref[...])
    o_ref[...] = _bf(ga * pa + gb * pb)


def _merge_call(h, ya, yb, w_gate, w_br_a, w_br_b):
    t, d = h.shape
    ka, kb = ya.shape[1], yb.shape[1]
    tm, tn = min(1024, t), 512
    nj = d // tn
    return pl.pallas_call(
        _merge_kernel,
        grid=(t // tm, nj),
        in_specs=[
            pl.BlockSpec((tm, d), lambda i, j: (i, 0)),
            pl.BlockSpec((tm, ka), lambda i, j: (i, 0)),
            pl.BlockSpec((tm, kb), lambda i, j: (i, 0)),
            pl.BlockSpec((d, tn), lambda i, j: (0, j)),
            pl.BlockSpec((d, tn), lambda i, j: (0, nj + j)),
            pl.BlockSpec((ka, tn), lambda i, j: (0, j)),
            pl.BlockSpec((kb, tn), lambda i, j: (0, j)),
        ],
        out_specs=pl.BlockSpec((tm, tn), lambda i, j: (i, j)),
        out_shape=jax.ShapeDtypeStruct((t, d), BF16),
        compiler_params=_params("parallel", "parallel"),
        name="gated_merge",
    )(h, ya, yb, w_gate, w_gate, w_br_a, w_br_b)


def _out_kernel(m_ref, w_ref, x_ref, gate_ref, g_ref, o_ref):
    out = _dot(m_ref[...], w_ref[...])
    y = out * lax.rsqrt(jnp.mean(out * out, axis=-1, keepdims=True) + RMS_EPS)
    o_ref[...] = x_ref[...] + gate_ref[...] * (y * g_ref[...])


def _out_call(merged, w_out, xf, gate, g_post, seq):
    t, d = xf.shape
    tm = 512
    per_seq = seq // tm
    return pl.pallas_call(
        _out_kernel,
        grid=(t // tm,),
        in_specs=[
            pl.BlockSpec((tm, d), lambda i: (i, 0)),
            pl.BlockSpec((d, d), lambda i: (0, 0)),
            pl.BlockSpec((tm, d), lambda i: (i, 0)),
            pl.BlockSpec((None, 1, d), lambda i: (i // per_seq, 0, 0)),
            pl.BlockSpec((1, d), lambda i: (0, 0)),
        ],
        out_specs=pl.BlockSpec((tm, d), lambda i: (i, 0)),
        out_shape=jax.ShapeDtypeStruct((t, d), F32),
        compiler_params=_params("parallel"),
        name="out_proj",
    )(merged, w_out, xf, gate, g_post)


def kernel(x, c, w_ada, b_ada, g_pre, w_in, w_pool, pool_scale, mu_shift, w_decay_up, w0, w_aaa_up, a0, w_mv_down, mu_mv, w_mv_up, mv0, k_k, k_a, r_k, lnx_g, lnx_b, w_br_a, w_br_b, w_out, g_post):
    bsz, seq, d = x.shape
    depth = w_ada.shape[0]
    pw = pool_scale.shape[1]
    rw = w0.shape[1]
    t = bsz * seq
    n_main = 2 * pw + 4 * rw
    n_lora = DECAY_LORA + AAA_LORA
    n_shift = 4 * rw + n_lora

    xf = x.reshape(t, d)
    c8 = jnp.pad(c, ((0, 8 - bsz), (0, 0)))
    mod = _mod_call(c8, w_ada, b_ada.reshape(depth, 1, 3 * d))

    v_first = None
    for l in range(depth):
        shift = mod[l, :bsz, :d].reshape(bsz, 1, d)
        scale = mod[l, :bsz, d:2 * d].reshape(bsz, 1, d)
        gate = mod[l, :bsz, 2 * d:].reshape(bsz, 1, d)

        w_gate = _bf(w_in[l, :, n_main + n_lora:])
        small_cols = [w_in[l, :, n_main:n_main + n_lora]]
        lora_rows = [w_decay_up[l], w_aaa_up[l]]
        mu_small = [mu_shift[l, 4 * rw:n_shift]]
        mv0_l = jnp.zeros((rw,), F32)
        if l > 0:
            small_cols.append(w_mv_down[l - 1])
            lora_rows.append(w_mv_up[l - 1])
            mu_small.append(mu_mv[l - 1])
            mv0_l = mv0[l - 1]
        w_small = jnp.concatenate(small_cols, axis=1)
        w_small = _bf(jnp.pad(w_small, ((0, 0), (0, SMALL_W - w_small.shape[1]))))
        w_lora = jnp.concatenate(lora_rows, axis=0)
        w_lora = _bf(jnp.pad(w_lora, ((0, SMALL_W - w_lora.shape[0]), (0, 0))))
        mu_small = jnp.concatenate(mu_small)
        mu_small = jnp.pad(mu_small, (0, SMALL_W - mu_small.shape[0])).reshape(1, SMALL_W)
        mu4 = mu_shift[l, :4 * rw].reshape(4, rw)
        pvec = jnp.concatenate(
            [mu4, jnp.stack([w0[l], a0[l], mv0_l, k_k[l], k_a[l], r_k[l].reshape(rw),
                             lnx_g[l], lnx_b[l]]),
             jnp.zeros((_P_ROWS - 12, rw), F32)], axis=0)

        h, small = _prenorm_call(xf, g_pre[l].reshape(1, d), scale, shift, w_small, seq)
        proj = _in_proj_call(h, w_in, l, n_main)
        proj3 = proj.reshape(bsz, seq, n_main)
        small3 = small.reshape(bsz, seq, SMALL_W)

        y_a = _pool_call(proj3, _bf(w_pool[l]), pool_scale[l].reshape(1, pw))
        y_b, v_first = _rwkv_call(proj3, small3, v_first, pvec, mu_small, w_lora)

        merged = _merge_call(h, y_a.reshape(t, pw), y_b.reshape(t, rw), w_gate,
                             _bf(w_br_a[l]), _bf(w_br_b[l]))
        xf = _out_call(merged, _bf(w_out[l]), xf, gate, g_post[l].reshape(1, d), seq)
    return xf.reshape(bsz, seq, d)
```

```python
import functools

import jax
import jax.numpy as jnp
from jax import lax
from jax.experimental import pallas as pl
from jax.experimental.pallas import tpu as pltpu

F32 = jnp.float32
BF16 = jnp.bfloat16

RMS_EPS = 1e-6
LNX_EPS = 64e-5
POOL_WINDOWS = (2, 4, 8, 16)
HEAD_DIM = 64
DECAY_LORA = 64
AAA_LORA = 64
MV_LORA = 32

CHUNK = 64
PAIR_HEADS = 2
PAIR_W = PAIR_HEADS * HEAD_DIM
RWKV_BLOCK_W = 1024
RWKV_SEQ_TILE = 512
RWKV_CHUNKS_PER_STEP = 4
SMALL_W = 256

_MIB = 1024 * 1024
_V7X_VMEM_LIMIT = 56 * _MIB

_EXP_M_HALF = 0.6065306597126334
_NT = (((1,), (1,)), ((), ()))
_TN = (((0,), (0,)), ((), ()))


def _params(*sem):
    return pltpu.CompilerParams(dimension_semantics=sem, vmem_limit_bytes=_V7X_VMEM_LIMIT)


def _dot(a, b):
    return jnp.dot(a, b, preferred_element_type=F32)


def _bf(x):
    return x.astype(BF16)


def _mod_kernel(c_ref, w_ref, b_ref, o_ref):
    cond = _bf(jax.nn.silu(c_ref[...]))
    o_ref[...] = _dot(cond, _bf(w_ref[...])) + b_ref[...]


def _mod_call(c8, w_ada, b_ada3):
    depth, d, n = w_ada.shape
    tn = 512
    return pl.pallas_call(
        _mod_kernel,
        grid=(depth, n // tn),
        in_specs=[
            pl.BlockSpec((8, d), lambda l, j: (0, 0)),
            pl.BlockSpec((None, d, tn), lambda l, j: (l, 0, j)),
            pl.BlockSpec((None, 1, tn), lambda l, j: (l, 0, j)),
        ],
        out_specs=pl.BlockSpec((None, 8, tn), lambda l, j: (l, 0, j)),
        out_shape=jax.ShapeDtypeStruct((depth, 8, n), F32),
        compiler_params=_params("parallel", "parallel"),
        name="adaln_mod",
    )(c8, w_ada, b_ada3)


def _prenorm_kernel(x_ref, g_ref, sc_ref, sh_ref, ws_ref, h_ref, sm_ref):
    x = x_ref[...]
    y = x * lax.rsqrt(jnp.mean(x * x, axis=-1, keepdims=True) + RMS_EPS)
    h = _bf((y * g_ref[...]) * (1.0 + sc_ref[...]) + sh_ref[...])
    h_ref[...] = h
    sm_ref[...] = _dot(h, ws_ref[...])


def _prenorm_call(xf, g, scale, shift, w_small, seq):
    t, d = xf.shape
    tm = 512
    per_seq = seq // tm
    return pl.pallas_call(
        _prenorm_kernel,
        grid=(t // tm,),
        in_specs=[
            pl.BlockSpec((tm, d), lambda i: (i, 0)),
            pl.BlockSpec((1, d), lambda i: (0, 0)),
            pl.BlockSpec((None, 1, d), lambda i: (i // per_seq, 0, 0)),
            pl.BlockSpec((None, 1, d), lambda i: (i // per_seq, 0, 0)),
            pl.BlockSpec((d, SMALL_W), lambda i: (0, 0)),
        ],
        out_specs=[
            pl.BlockSpec((tm, d), lambda i: (i, 0)),
            pl.BlockSpec((tm, SMALL_W), lambda i: (i, 0)),
        ],
        out_shape=[
            jax.ShapeDtypeStruct((t, d), BF16),
            jax.ShapeDtypeStruct((t, SMALL_W), F32),
        ],
        compiler_params=_params("parallel"),
        name="prenorm",
    )(xf, g, scale, shift, w_small)


def _in_proj_kernel(h_ref, w_ref, o_ref, w_scr):
    @pl.when(pl.program_id(1) == 0)
    def _():
        w_scr[...] = _bf(w_ref[...])

    o_ref[...] = _dot(h_ref[...], w_scr[...])


def _in_proj_call(h, w_in, layer, n_cols):
    m, k = h.shape
    tm, tn = min(1024, m), 1024
    return pl.pallas_call(
        _in_proj_kernel,
        grid=(n_cols // tn, m // tm),
        in_specs=[
            pl.BlockSpec((tm, k), lambda j, i: (i, 0)),
            pl.BlockSpec((None, k, tn), lambda j, i: (layer, 0, j)),
        ],
        out_specs=pl.BlockSpec((tm, tn), lambda j, i: (i, j)),
        out_shape=jax.ShapeDtypeStruct((m, n_cols), F32),
        scratch_shapes=[pltpu.VMEM((k, tn), BF16)],
        compiler_params=_params("parallel", "arbitrary"),
        name="in_proj",
    )(h, w_in)


def _pool_kernel(u_ref, z_ref, w_ref, ps_ref, y_ref):
    group = pl.program_id(1)
    u = u_ref[...]
    row = lax.broadcasted_iota(jnp.int32, (u.shape[0], 1), 0)

    def shifted(a, s):
        return jnp.where(row >= s, pltpu.roll(a, s, 0), 0.0)

    for gi, window in enumerate(POOL_WINDOWS):

        @pl.when(group == gi)
        def _():
            acc, span = u, 1
            while span < window:
                acc = acc + shifted(acc, span)
                span *= 2
            cnt = jnp.minimum(row + 1, window).astype(F32)
            pooled = acc / cnt - u
            y = _dot(_bf(pooled), w_ref[...])
            y_ref[...] = _bf(y * ps_ref[...] * jax.nn.silu(z_ref[...]))


def _pool_call(proj3, w_pool, pool_scale):
    b, s, _ = proj3.shape
    groups, gd, _ = w_pool.shape
    return pl.pallas_call(
        _pool_kernel,
        grid=(b, groups),
        in_specs=[
            pl.BlockSpec((None, s, gd), lambda i, g: (i, 0, g)),
            pl.BlockSpec((None, s, gd), lambda i, g: (i, 0, groups + g)),
            pl.BlockSpec((None, gd, gd), lambda i, g: (g, 0, 0)),
            pl.BlockSpec((1, gd), lambda i, g: (0, g)),
        ],
        out_specs=pl.BlockSpec((None, s, gd), lambda i, g: (i, 0, g)),
        out_shape=jax.ShapeDtypeStruct((b, s, groups * gd), BF16),
        compiler_params=_params("parallel", "parallel"),
        name="pool_mixer",
    )(proj3, proj3, w_pool, pool_scale)


(_P_MU_R, _P_MU_K, _P_MU_V, _P_MU_Z, _P_W0, _P_A0, _P_MV0, _P_KK, _P_KA, _P_RK,
 _P_LNG, _P_LNB) = range(12)
_P_ROWS = 16


def _split_terms(x, terms):
    pieces, rem = [], x
    for _ in range(terms):
        p = _bf(rem)
        pieces.append(p)
        rem = rem - p.astype(F32)
    return pieces


def _rwkv_kernel(has_mix, *refs):
    if has_mix:
        (r_ref, k_ref, v_ref, z_ref, sm_ref, vf_ref, pv_ref, mus_ref, wl_ref,
         y_ref, state_scr, carry_scr, carry_sm_scr) = refs
    else:
        (r_ref, k_ref, v_ref, z_ref, sm_ref, pv_ref, mus_ref, wl_ref,
         y_ref, vout_ref, state_scr, carry_scr, carry_sm_scr) = refs
    ts, wb = r_ref.shape
    c_, pw, st = CHUNK, PAIR_W, PAIR_HEADS * CHUNK
    shifted_refs = (r_ref, k_ref, v_ref, z_ref)

    @pl.when(pl.program_id(2) == 0)
    def _():
        state_scr[...] = jnp.zeros_like(state_scr)
        carry_scr[...] = jnp.zeros_like(carry_scr)
        carry_sm_scr[...] = jnp.zeros_like(carry_sm_scr)

    pv = pv_ref[...]
    mus = mus_ref[...]

    def iota(shape, dim):
        return lax.broadcasted_iota(jnp.int32, shape, dim)

    first_row = iota((c_, pw), 0) == 0
    first_row_sm = iota((c_, SMALL_W), 0) == 0
    lane_sm = iota((c_, SMALL_W), 1)
    m_wlo = lane_sm < DECAY_LORA
    m_alo = (lane_sm >= DECAY_LORA) & (lane_sm < DECAY_LORA + AAA_LORA)
    m_mv = (lane_sm >= DECAY_LORA + AAA_LORA) & (lane_sm < DECAY_LORA + AAA_LORA + MV_LORA)
    lane = iota((c_, pw), 1)
    head_masks = [lane // HEAD_DIM == j for j in range(PAIR_HEADS)]
    sq_r, sq_c = iota((st, pw), 0), iota((st, pw), 1)
    blockdiag = sq_r // HEAD_DIM == sq_c // HEAD_DIM
    ones_bd = jnp.where(blockdiag, 1.0, 0.0).astype(BF16)
    ones_bd2 = jnp.concatenate([ones_bd, ones_bd], axis=0)
    tok_r, tok_c = iota((c_, 2 * st), 0), iota((c_, 2 * st), 1) % c_
    strict_lower = (tok_r > tok_c)[:, :st]
    lower = tok_r >= tok_c
    eye = jnp.where(tok_r == tok_c, 1.0, 0.0).astype(F32)[:, :st]
    tri = jnp.where(iota((c_, c_), 0) >= iota((c_, c_), 1), 1.0, 0.0).astype(BF16)
    tri2 = jnp.concatenate([tri, tri], axis=1)

    def segsum(x):
        return _dot(jnp.concatenate(_split_terms(x, 2), axis=1), ones_bd2)

    def stack(x):
        return _bf(jnp.concatenate([jnp.where(m, x, 0.0) for m in head_masks], axis=0))

    def blockdiag_of(x):
        return _bf(jnp.where(blockdiag, jnp.concatenate([x] * PAIR_HEADS, axis=0), 0.0))

    def load_pair(c, p):
        lanes = slice(p * pw, (p + 1) * pw)
        r0 = pl.multiple_of(c * c_, c_)
        rows = pl.ds(r0, c_)
        prev_rows = pl.ds(pl.multiple_of(jnp.maximum(r0 - 8, 0), 8), 8)
        cur = [ref[rows, lanes] for ref in shifted_refs]
        last = [jnp.where(c > 0, ref[prev_rows, lanes], carry_scr[i, :, lanes])[7:8]
                for i, ref in enumerate(shifted_refs)]
        vf = vf_ref[rows, lanes] if has_mix else None
        return cur, last, vf

    def pair_chunk(p, loaded, lw, la, lm, states, state_delay):
        lanes = slice(p * pw, (p + 1) * pw)
        cur, last, vf = loaded
        prow = lambda i: pv[i:i + 1, lanes]

        def tshift(i, mu):
            prev = jnp.where(first_row, last[i], pltpu.roll(cur[i], 1, 0))
            return cur[i] + (prev - cur[i]) * mu

        xr = tshift(0, prow(_P_MU_R))
        xk = tshift(1, prow(_P_MU_K))
        xv = tshift(2, prow(_P_MU_V))
        xz = tshift(3, prow(_P_MU_Z))
        v_shifted = xv

        ld = -_EXP_M_HALF * jax.nn.sigmoid(prow(_P_W0) + lw)
        asig = jax.nn.sigmoid(prow(_P_A0) + la)
        if has_mix:
            xv = xv + (vf - xv) * jax.nn.sigmoid(prow(_P_MV0) + lm)
        kk = xk * prow(_P_KK)
        kp = xk * (1.0 + (asig - 1.0) * prow(_P_KA))
        cum = _dot(tri2, jnp.concatenate(_split_terms(ld, 2), axis=0))
        kk_sq, rk_sum = yield [kk * kk, xr * kp * prow(_P_RK)]
        kkn = kk * lax.rsqrt(jnp.maximum(kk_sq, 1e-24))
        a = -kkn
        b = kkn * asig
        bonus = rk_sum * xv
        e_in = jnp.exp(cum)
        e_out = jnp.exp(-cum)
        w_end = e_in[c_ - 1:c_, :]
        e_end = w_end * e_out
        a_t = a * jnp.exp(cum - ld)
        r_t = xr * e_in
        b_t = b * e_out
        k_t = kp * e_out
        b_c = _bf(b * e_end)
        k_c = _bf(kp * e_end)
        v_st = stack(xv)

        g = lax.dot_general(_bf(jnp.concatenate([a_t, r_t], axis=0)),
                            jnp.concatenate([stack(b_t), stack(k_t)], axis=0), _NT,
                            preferred_element_type=F32)
        a_ab = jnp.where(strict_lower, g[:c_, :st], 0.0)
        a_ak = jnp.where(strict_lower, g[:c_, st:], 0.0)
        a_r = _bf(jnp.where(lower, g[c_:, :], 0.0))
        yield

        akv = _dot(_bf(a_ak), v_st)
        t_inv = eye + a_ab
        a_pow = _dot(_bf(a_ab), blockdiag_of(a_ab))
        yield
        span = 2
        while span < c_:
            pow_bd = blockdiag_of(a_pow)
            span *= 2
            if span < c_:
                both = _dot(_bf(jnp.concatenate([t_inv, a_pow], axis=0)), pow_bd)
                t_inv, a_pow = t_inv + both[:c_], both[c_:]
            else:
                t_inv = t_inv + _dot(_bf(t_inv), pow_bd)
            yield

        x1 = _dot(_bf(t_inv), jnp.concatenate([stack(a_t), stack(akv)], axis=1))
        yield
        a_p, u0 = x1[:, :pw], x1[:, pw:]
        x2 = _dot(a_r[:, :st], jnp.concatenate([stack(a_p), stack(u0)], axis=1))
        x3 = _dot(a_r[:, st:], v_st)
        yield
        r_p = r_t + x2[:, :pw]
        y0 = x2[:, pw:] + x3
        for _ in range(state_delay):
            yield

        state = states[p]
        ou = lax.dot_general(_bf(jnp.concatenate([r_p, a_p], axis=0)), _bf(state), _NT,
                             preferred_element_type=F32)
        yield
        o = ou[:c_] + y0
        u = ou[c_:] + u0
        states[p] = state * w_end + jnp.where(
            blockdiag,
            lax.dot_general(_bf(jnp.concatenate([u, xv], axis=0)),
                            jnp.concatenate([b_c, k_c], axis=0), _TN,
                            preferred_element_type=F32), 0.0)

        (sum_o,) = yield [o]
        d = o - sum_o * (1.0 / HEAD_DIM)
        (sum_dd,) = yield [d * d]
        var = sum_dd * (1.0 / HEAD_DIM)
        o = d * lax.rsqrt(var + LNX_EPS) * prow(_P_LNG) + prow(_P_LNB)
        return _bf((o + bonus) * jax.nn.silu(xz)), v_shifted

    def lora_up(c):
        r0 = pl.multiple_of(c * c_, c_)
        cur = sm_ref[pl.ds(r0, c_), :]
        prev_rows = pl.ds(pl.multiple_of(jnp.maximum(r0 - 8, 0), 8), 8)
        last = jnp.where(c > 0, sm_ref[prev_rows, :], carry_sm_scr[...])[7:8]
        prev = jnp.where(first_row_sm, last, pltpu.roll(cur, 1, 0))
        sms = cur + (prev - cur) * mus
        wl = wl_ref[...]
        lw = _dot(_bf(jnp.where(m_wlo, jnp.tanh(sms), 0.0)), wl)
        la = _dot(_bf(jnp.where(m_alo, sms, 0.0)), wl)
        lm = _dot(_bf(jnp.where(m_mv, sms, 0.0)), wl) if has_mix else None
        return lw, la, lm

    pairs = range(wb // pw)
    lane_blk = lambda x, p: None if x is None else x[:, p * pw:(p + 1) * pw]

    def body(step, carry):
        states = [state_scr[p] for p in pairs]
        chunks = [step * RWKV_CHUNKS_PER_STEP + j for j in range(RWKV_CHUNKS_PER_STEP)]
        chains = []
        for j, c in enumerate(chunks):
            lw, la, lm = lora_up(c)
            chains += [pair_chunk(p, load_pair(c, p), lane_blk(lw, p), lane_blk(la, p),
                                  lane_blk(lm, p), states, 2 * j) for p in pairs]
        results = [None] * len(chains)
        replies = [None] * len(chains)
        while any(r is None for r in results):
            requests = []
            for i, chain in enumerate(chains):
                if results[i] is not None:
                    continue
                try:
                    wanted = chain.send(replies[i])
                except StopIteration as done:
                    results[i] = done.value
                    continue
                replies[i] = None
                if wanted is not None:
                    requests.append((i, wanted))
            if requests:
                sums = segsum(jnp.concatenate([x for _, xs in requests for x in xs], axis=0))
                row = 0
                for i, xs in requests:
                    replies[i] = [sums[row + k * c_:row + (k + 1) * c_] for k in range(len(xs))]
                    row += len(xs) * c_
        for i, (y, v_shifted) in enumerate(results):
            c, p = chunks[i // len(pairs)], i % len(pairs)
            rows = pl.ds(pl.multiple_of(c * c_, c_), c_)
            lanes = slice(p * pw, (p + 1) * pw)
            y_ref[rows, lanes] = y
            if not has_mix:
                vout_ref[rows, lanes] = v_shifted
        for p in pairs:
            state_scr[p] = states[p]
        return carry

    lax.fori_loop(0, ts // (c_ * RWKV_CHUNKS_PER_STEP), body, 0)

    for i, ref in enumerate(shifted_refs):
        carry_scr[i] = ref[ts - 8:ts, :]
    carry_sm_scr[...] = sm_ref[ts - 8:ts, :]


def _rwkv_call(proj3, small3, v_first, pvec, mu_small, w_lora):
    b, s, _ = proj3.shape
    rw = pvec.shape[1]
    wb, ts = min(RWKV_BLOCK_W, rw), min(RWKV_SEQ_TILE, s)
    groups = rw // wb
    has_mix = v_first is not None
    col0 = 2 * rw // wb

    def col_spec(offset):
        return pl.BlockSpec((None, ts, wb), lambda i, g, j: (i, j, offset + g))

    in_specs = [col_spec(col0), col_spec(col0 + groups), col_spec(col0 + 2 * groups),
                col_spec(col0 + 3 * groups),
                pl.BlockSpec((None, ts, SMALL_W), lambda i, g, j: (i, j, 0))]
    args = [proj3, proj3, proj3, proj3, small3]
    if has_mix:
        in_specs.append(col_spec(0))
        args.append(v_first)
    in_specs += [
        pl.BlockSpec((_P_ROWS, wb), lambda i, g, j: (0, g)),
        pl.BlockSpec((1, SMALL_W), lambda i, g, j: (0, 0)),
        pl.BlockSpec((SMALL_W, wb), lambda i, g, j: (0, g)),
    ]
    args += [pvec, mu_small, w_lora]
    out_specs = [col_spec(0)]
    out_shape = [jax.ShapeDtypeStruct((b, s, rw), BF16)]
    if not has_mix:
        out_specs.append(col_spec(0))
        out_shape.append(jax.ShapeDtypeStruct((b, s, rw), F32))
    res = pl.pallas_call(
        functools.partial(_rwkv_kernel, has_mix),
        grid=(b, groups, s // ts),
        in_specs=in_specs,
        out_specs=out_specs,
        out_shape=out_shape,
        scratch_shapes=[
            pltpu.VMEM((wb // PAIR_W, PAIR_W, PAIR_W), F32),
            pltpu.VMEM((4, 8, wb), F32),
            pltpu.VMEM((8, SMALL_W), F32),
        ],
        compiler_params=_params("parallel", "parallel", "arbitrary"),
        name="rwkv_mixer",
    )(*args)
    return (res[0], v_first) if has_mix else (res[0], res[1])


def _merge_kernel(h_ref, ya_ref, yb_ref, wga_ref, wgb_ref, wa_ref, wb_ref, o_ref,
                  wga_scr, wgb_scr, wa_scr, wb_scr):
    @pl.when(pl.program_id(1) == 0)
    def _():
        wga_scr[...] = _bf(wga_ref[...])
        wgb_scr[...] = _bf(wgb_ref[...])
        wa_scr[...] = _bf(wa_ref[...])
        wb_scr[...] = _bf(wb_ref[...])

    h = h_ref[...]
    ga = jax.nn.sigmoid(_dot(h, wga_scr[...]))
    gb = jax.nn.sigmoid(_dot(h, wgb_scr[...]))
    pa = _dot(ya_ref[...], wa_scr[...])
    pb = _dot(yb_ref[...], wb_scr[...])
    o_ref[...] = _bf(ga * pa + gb * pb)


def _merge_call(h, ya, yb, w_gate, w_br_a, w_br_b, layer):
    t, d = h.shape
    ka, kb = ya.shape[1], yb.shape[1]
    tm, tn = min(1024, t), 512
    nj = d // tn
    return pl.pallas_call(
        _merge_kernel,
        grid=(nj, t // tm),
        in_specs=[
            pl.BlockSpec((tm, d), lambda j, i: (i, 0)),
            pl.BlockSpec((tm, ka), lambda j, i: (i, 0)),
            pl.BlockSpec((tm, kb), lambda j, i: (i, 0)),
            pl.BlockSpec((d, tn), lambda j, i: (0, j)),
            pl.BlockSpec((d, tn), lambda j, i: (0, nj + j)),
            pl.BlockSpec((None, ka, tn), lambda j, i: (layer, 0, j)),
            pl.BlockSpec((None, kb, tn), lambda j, i: (layer, 0, j)),
        ],
        out_specs=pl.BlockSpec((tm, tn), lambda j, i: (i, j)),
        out_shape=jax.ShapeDtypeStruct((t, d), BF16),
        scratch_shapes=[pltpu.VMEM((d, tn), BF16), pltpu.VMEM((d, tn), BF16),
                        pltpu.VMEM((ka, tn), BF16), pltpu.VMEM((kb, tn), BF16)],
        compiler_params=_params("parallel", "arbitrary"),
        name="gated_merge",
    )(h, ya, yb, w_gate, w_gate, w_br_a, w_br_b)


def _out_kernel(m_ref, w_ref, x_ref, gate_ref, g_ref, o_ref, w_scr):
    @pl.when(pl.program_id(0) == 0)
    def _():
        w_scr[...] = _bf(w_ref[...])

    out = _dot(m_ref[...], w_scr[...])
    y = out * lax.rsqrt(jnp.mean(out * out, axis=-1, keepdims=True) + RMS_EPS)
    o_ref[...] = x_ref[...] + gate_ref[...] * (y * g_ref[...])


def _out_call(merged, w_out, layer, xf, gate, g_post, seq):
    t, d = xf.shape
    tm = 512
    per_seq = seq // tm
    return pl.pallas_call(
        _out_kernel,
        grid=(t // tm,),
        in_specs=[
            pl.BlockSpec((tm, d), lambda i: (i, 0)),
            pl.BlockSpec((None, d, d), lambda i: (layer, 0, 0), pipeline_mode=pl.Buffered(1)),
            pl.BlockSpec((tm, d), lambda i: (i, 0)),
            pl.BlockSpec((None, 1, d), lambda i: (i // per_seq, 0, 0)),
            pl.BlockSpec((1, d), lambda i: (0, 0)),
        ],
        out_specs=pl.BlockSpec((tm, d), lambda i: (i, 0)),
        out_shape=jax.ShapeDtypeStruct((t, d), F32),
        scratch_shapes=[pltpu.VMEM((d, d), BF16)],
        compiler_params=_params("arbitrary"),
        name="out_proj",
    )(merged, w_out, xf, gate, g_post)


def kernel(x, c, w_ada, b_ada, g_pre, w_in, w_pool, pool_scale, mu_shift, w_decay_up, w0, w_aaa_up, a0, w_mv_down, mu_mv, w_mv_up, mv0, k_k, k_a, r_k, lnx_g, lnx_b, w_br_a, w_br_b, w_out, g_post):
    bsz, seq, d = x.shape
    depth = w_ada.shape[0]
    pw = pool_scale.shape[1]
    rw = w0.shape[1]
    t = bsz * seq
    n_main = 2 * pw + 4 * rw
    n_lora = DECAY_LORA + AAA_LORA
    n_shift = 4 * rw + n_lora

    xf = x.reshape(t, d)
    c8 = jnp.pad(c, ((0, 8 - bsz), (0, 0)))
    mod = _mod_call(c8, w_ada, b_ada.reshape(depth, 1, 3 * d))

    v_first = None
    for l in range(depth):
        shift = mod[l, :bsz, :d].reshape(bsz, 1, d)
        scale = mod[l, :bsz, d:2 * d].reshape(bsz, 1, d)
        gate = mod[l, :bsz, 2 * d:].reshape(bsz, 1, d)

        w_gate = w_in[l, :, n_main + n_lora:]
        small_cols = [w_in[l, :, n_main:n_main + n_lora]]
        lora_rows = [w_decay_up[l], w_aaa_up[l]]
        mu_small = [mu_shift[l, 4 * rw:n_shift]]
        mv0_l = jnp.zeros((rw,), F32)
        if l > 0:
            small_cols.append(w_mv_down[l - 1])
            lora_rows.append(w_mv_up[l - 1])
            mu_small.append(mu_mv[l - 1])
            mv0_l = mv0[l - 1]
        w_small = jnp.concatenate(small_cols, axis=1)
        w_small = _bf(jnp.pad(w_small, ((0, 0), (0, SMALL_W - w_small.shape[1]))))
        w_lora = jnp.concatenate(lora_rows, axis=0)
        w_lora = _bf(jnp.pad(w_lora, ((0, SMALL_W - w_lora.shape[0]), (0, 0))))
        mu_small = jnp.concatenate(mu_small)
        mu_small = jnp.pad(mu_small, (0, SMALL_W - mu_small.shape[0])).reshape(1, SMALL_W)
        mu4 = mu_shift[l, :4 * rw].reshape(4, rw)
        pvec = jnp.concatenate(
            [mu4, jnp.stack([w0[l], a0[l], mv0_l, k_k[l], k_a[l], r_k[l].reshape(rw),
                             lnx_g[l], lnx_b[l]]),
             jnp.zeros((_P_ROWS - 12, rw), F32)], axis=0)

        h, small = _prenorm_call(xf, g_pre[l].reshape(1, d), scale, shift, w_small, seq)
        proj = _in_proj_call(h, w_in, l, n_main)
        proj3 = proj.reshape(bsz, seq, n_main)
        small3 = small.reshape(bsz, seq, SMALL_W)

        y_a = _pool_call(proj3, _bf(w_pool[l]), pool_scale[l].reshape(1, pw))
        y_b, v_first = _rwkv_call(proj3, small3, v_first, pvec, mu_small, w_lora)

        merged = _merge_call(h, y_a.reshape(t, pw), y_b.reshape(t, rw), w_gate,
                             w_br_a, w_br_b, l)
        xf = _out_call(merged, w_out, l, xf, gate, g_post[l].reshape(1, d), seq)
    return xf.reshape(bsz, seq, d)
```

```python
import functools

import jax
import jax.numpy as jnp
from jax import lax
from jax.experimental import pallas as pl
from jax.experimental.pallas import tpu as pltpu

F32 = jnp.float32
BF16 = jnp.bfloat16

RMS_EPS = 1e-6
LNX_EPS = 64e-5
POOL_WINDOWS = (2, 4, 8, 16)
HEAD_DIM = 64
DECAY_LORA = 64
AAA_LORA = 64
MV_LORA = 32

CHUNK = 64
PAIR_HEADS = 2
PAIR_W = PAIR_HEADS * HEAD_DIM
RWKV_BLOCK_W = 1024
RWKV_SEQ_TILE = 512
RWKV_CHUNKS_PER_STEP = 4
SMALL_W = 256

_LANES = 128
_MIB = 1024 * 1024
_V7X_VMEM_LIMIT = 56 * _MIB

_EXP_M_HALF = 0.6065306597126334
_NT = (((1,), (1,)), ((), ()))
_TN = (((0,), (0,)), ((), ()))


def _params(*sem):
    return pltpu.CompilerParams(dimension_semantics=sem, vmem_limit_bytes=_V7X_VMEM_LIMIT)


def _dot(a, b):
    return jnp.dot(a, b, preferred_element_type=F32)


def _bf(x):
    return x.astype(BF16)


def _mod_kernel(c_ref, w_ref, b_ref, o_ref):
    cond = _bf(jax.nn.silu(c_ref[...]))
    o_ref[...] = _dot(cond, _bf(w_ref[...])) + b_ref[...]


def _mod_call(c8, w_ada, b_ada3):
    depth, d, n = w_ada.shape
    tn = 512
    return pl.pallas_call(
        _mod_kernel,
        grid=(depth, n // tn),
        in_specs=[
            pl.BlockSpec((8, d), lambda l, j: (0, 0)),
            pl.BlockSpec((None, d, tn), lambda l, j: (l, 0, j)),
            pl.BlockSpec((None, 1, tn), lambda l, j: (l, 0, j)),
        ],
        out_specs=pl.BlockSpec((None, 8, tn), lambda l, j: (l, 0, j)),
        out_shape=jax.ShapeDtypeStruct((depth, 8, n), F32),
        compiler_params=_params("parallel", "parallel"),
        name="adaln_mod",
    )(c8, w_ada, b_ada3)


def _prenorm_kernel(x_ref, g_ref, sc_ref, sh_ref, ws_ref, h_ref, sm_ref):
    x = x_ref[...]
    y = x * lax.rsqrt(jnp.mean(x * x, axis=-1, keepdims=True) + RMS_EPS)
    h = _bf((y * g_ref[...]) * (1.0 + sc_ref[...]) + sh_ref[...])
    h_ref[...] = h
    sm_ref[...] = _dot(h, _bf(ws_ref[...]))


def _prenorm_call(xf, g, scale, shift, w_small, seq):
    t, d = xf.shape
    tm = 512
    per_seq = seq // tm
    return pl.pallas_call(
        _prenorm_kernel,
        grid=(t // tm,),
        in_specs=[
            pl.BlockSpec((tm, d), lambda i: (i, 0)),
            pl.BlockSpec((1, d), lambda i: (0, 0)),
            pl.BlockSpec((None, 1, d), lambda i: (i // per_seq, 0, 0)),
            pl.BlockSpec((None, 1, d), lambda i: (i // per_seq, 0, 0)),
            pl.BlockSpec((d, SMALL_W), lambda i: (0, 0)),
        ],
        out_specs=[
            pl.BlockSpec((tm, d), lambda i: (i, 0)),
            pl.BlockSpec((tm, SMALL_W), lambda i: (i, 0)),
        ],
        out_shape=[
            jax.ShapeDtypeStruct((t, d), BF16),
            jax.ShapeDtypeStruct((t, SMALL_W), F32),
        ],
        compiler_params=_params("parallel"),
        name="prenorm",
    )(xf, g, scale, shift, w_small)


def _in_proj_kernel(h_ref, w_ref, o_ref, w_scr):
    @pl.when(pl.program_id(1) == 0)
    def _():
        w_scr[...] = _bf(w_ref[...])

    o_ref[...] = _dot(h_ref[...], w_scr[...])


def _in_proj_call(h, w_in, layer, n_cols):
    m, k = h.shape
    tm, tn = min(1024, m), 1024
    return pl.pallas_call(
        _in_proj_kernel,
        grid=(n_cols // tn, m // tm),
        in_specs=[
            pl.BlockSpec((tm, k), lambda j, i: (i, 0)),
            pl.BlockSpec((None, k, tn), lambda j, i: (layer, 0, j)),
        ],
        out_specs=pl.BlockSpec((tm, tn), lambda j, i: (i, j)),
        out_shape=jax.ShapeDtypeStruct((m, n_cols), F32),
        scratch_shapes=[pltpu.VMEM((k, tn), BF16)],
        compiler_params=_params("parallel", "arbitrary"),
        name="in_proj",
    )(h, w_in)


def _pool_kernel(u_ref, z_ref, w_ref, ps_ref, y_ref):
    group = pl.program_id(1)
    u = u_ref[...]
    row = lax.broadcasted_iota(jnp.int32, (u.shape[0], 1), 0)

    def shifted(a, s):
        return jnp.where(row >= s, pltpu.roll(a, s, 0), 0.0)

    for gi, window in enumerate(POOL_WINDOWS):

        @pl.when(group == gi)
        def _():
            acc, span = u, 1
            while span < window:
                acc = acc + shifted(acc, span)
                span *= 2
            cnt = jnp.minimum(row + 1, window).astype(F32)
            pooled = acc / cnt - u
            y = _dot(_bf(pooled), w_ref[...])
            y_ref[...] = _bf(y * ps_ref[...] * jax.nn.silu(z_ref[...]))


def _pool_call(proj3, w_pool, pool_scale):
    b, s, _ = proj3.shape
    groups, gd, _ = w_pool.shape
    return pl.pallas_call(
        _pool_kernel,
        grid=(b, groups),
        in_specs=[
            pl.BlockSpec((None, s, gd), lambda i, g: (i, 0, g)),
            pl.BlockSpec((None, s, gd), lambda i, g: (i, 0, groups + g)),
            pl.BlockSpec((None, gd, gd), lambda i, g: (g, 0, 0)),
            pl.BlockSpec((1, gd), lambda i, g: (0, g)),
        ],
        out_specs=pl.BlockSpec((None, s, gd), lambda i, g: (i, 0, g)),
        out_shape=jax.ShapeDtypeStruct((b, s, groups * gd), BF16),
        compiler_params=_params("parallel", "parallel"),
        name="pool_mixer",
    )(proj3, proj3, w_pool, pool_scale)


(_P_MU_R, _P_MU_K, _P_MU_V, _P_MU_Z, _P_W0, _P_A0, _P_MV0, _P_KK, _P_KA, _P_RK,
 _P_LNG, _P_LNB) = range(12)
_P_ROWS = 16


def _split_terms(x, terms):
    pieces, rem = [], x
    for _ in range(terms):
        p = _bf(rem)
        pieces.append(p)
        rem = rem - p.astype(F32)
    return pieces


def _rwkv_kernel(has_mix, *refs):
    if has_mix:
        (r_ref, k_ref, v_ref, z_ref, sm_ref, vf_ref, pv_ref, mus_ref, wl_ref,
         y_ref, state_scr, carry_scr, carry_sm_scr) = refs
    else:
        (r_ref, k_ref, v_ref, z_ref, sm_ref, pv_ref, mus_ref, wl_ref,
         y_ref, vout_ref, state_scr, carry_scr, carry_sm_scr) = refs
    ts, wb = r_ref.shape
    c_, pw, st = CHUNK, PAIR_W, PAIR_HEADS * CHUNK
    shifted_refs = (r_ref, k_ref, v_ref, z_ref)

    @pl.when(pl.program_id(2) == 0)
    def _():
        state_scr[...] = jnp.zeros_like(state_scr)
        carry_scr[...] = jnp.zeros_like(carry_scr)
        carry_sm_scr[...] = jnp.zeros_like(carry_sm_scr)

    pv = pv_ref[...]
    mus = mus_ref[...]

    def iota(shape, dim):
        return lax.broadcasted_iota(jnp.int32, shape, dim)

    first_row = iota((c_, pw), 0) == 0
    first_row_sm = iota((c_, SMALL_W), 0) == 0
    lane_sm = iota((c_, SMALL_W), 1)
    m_wlo = lane_sm < DECAY_LORA
    m_alo = (lane_sm >= DECAY_LORA) & (lane_sm < DECAY_LORA + AAA_LORA)
    m_mv = (lane_sm >= DECAY_LORA + AAA_LORA) & (lane_sm < DECAY_LORA + AAA_LORA + MV_LORA)
    lane = iota((c_, pw), 1)
    head_masks = [lane // HEAD_DIM == j for j in range(PAIR_HEADS)]
    sq_r, sq_c = iota((st, pw), 0), iota((st, pw), 1)
    blockdiag = sq_r // HEAD_DIM == sq_c // HEAD_DIM
    ones_bd = jnp.where(blockdiag, 1.0, 0.0).astype(BF16)
    ones_bd2 = jnp.concatenate([ones_bd, ones_bd], axis=0)
    tok_r, tok_c = iota((c_, 2 * st), 0), iota((c_, 2 * st), 1) % c_
    strict_lower = (tok_r > tok_c)[:, :st]
    lower = tok_r >= tok_c
    eye = jnp.where(tok_r == tok_c, 1.0, 0.0).astype(F32)[:, :st]
    tri = jnp.where(iota((c_, c_), 0) >= iota((c_, c_), 1), 1.0, 0.0).astype(BF16)
    tri2 = jnp.concatenate([tri, tri], axis=1)

    def segsum(x):
        return _dot(jnp.concatenate(_split_terms(x, 2), axis=1), ones_bd2)

    def stack(x):
        return _bf(jnp.concatenate([jnp.where(m, x, 0.0) for m in head_masks], axis=0))

    def blockdiag_of(x):
        return _bf(jnp.where(blockdiag, jnp.concatenate([x] * PAIR_HEADS, axis=0), 0.0))

    def load_pair(c, p):
        lanes = slice(p * pw, (p + 1) * pw)
        r0 = pl.multiple_of(c * c_, c_)
        rows = pl.ds(r0, c_)
        prev_rows = pl.ds(pl.multiple_of(jnp.maximum(r0 - 8, 0), 8), 8)
        cur = [ref[rows, lanes] for ref in shifted_refs]
        last = [jnp.where(c > 0, ref[prev_rows, lanes], carry_scr[i, :, lanes])[7:8]
                for i, ref in enumerate(shifted_refs)]
        vf = vf_ref[rows, lanes] if has_mix else None
        return cur, last, vf

    def pair_chunk(p, loaded, lw, la, lm, states, state_delay):
        lanes = slice(p * pw, (p + 1) * pw)
        cur, last, vf = loaded
        prow = lambda i: pv[i:i + 1, lanes]

        def tshift(i, mu):
            prev = jnp.where(first_row, last[i], pltpu.roll(cur[i], 1, 0))
            return cur[i] + (prev - cur[i]) * mu

        xr = tshift(0, prow(_P_MU_R))
        xk = tshift(1, prow(_P_MU_K))
        xv = tshift(2, prow(_P_MU_V))
        xz = tshift(3, prow(_P_MU_Z))
        v_shifted = xv

        ld = -_EXP_M_HALF * jax.nn.sigmoid(prow(_P_W0) + lw)
        asig = jax.nn.sigmoid(prow(_P_A0) + la)
        if has_mix:
            xv = xv + (vf - xv) * jax.nn.sigmoid(prow(_P_MV0) + lm)
        kk = xk * prow(_P_KK)
        kp = xk * (1.0 + (asig - 1.0) * prow(_P_KA))
        cum = _dot(tri2, jnp.concatenate(_split_terms(ld, 2), axis=0))
        kk_sq, rk_sum = yield [kk * kk, xr * kp * prow(_P_RK)]
        kkn = kk * lax.rsqrt(jnp.maximum(kk_sq, 1e-24))
        a = -kkn
        b = kkn * asig
        bonus = rk_sum * xv
        e_in = jnp.exp(cum)
        e_out = jnp.exp(-cum)
        w_end = e_in[c_ - 1:c_, :]
        e_end = w_end * e_out
        a_t = a * jnp.exp(cum - ld)
        r_t = xr * e_in
        b_t = b * e_out
        k_t = kp * e_out
        b_c = _bf(b * e_end)
        k_c = _bf(kp * e_end)
        v_st = stack(xv)

        g = lax.dot_general(_bf(jnp.concatenate([a_t, r_t], axis=0)),
                            jnp.concatenate([stack(b_t), stack(k_t)], axis=0), _NT,
                            preferred_element_type=F32)
        a_ab = jnp.where(strict_lower, g[:c_, :st], 0.0)
        a_ak = jnp.where(strict_lower, g[:c_, st:], 0.0)
        a_r = _bf(jnp.where(lower, g[c_:, :], 0.0))
        yield

        akv = _dot(_bf(a_ak), v_st)
        t_inv = eye + a_ab
        a_pow = _dot(_bf(a_ab), blockdiag_of(a_ab))
        yield
        span = 2
        while span < c_:
            pow_bd = blockdiag_of(a_pow)
            span *= 2
            if span < c_:
                both = _dot(_bf(jnp.concatenate([t_inv, a_pow], axis=0)), pow_bd)
                t_inv, a_pow = t_inv + both[:c_], both[c_:]
            else:
                t_inv = t_inv + _dot(_bf(t_inv), pow_bd)
            yield

        x1 = _dot(_bf(t_inv), jnp.concatenate([stack(a_t), stack(akv)], axis=1))
        yield
        a_p, u0 = x1[:, :pw], x1[:, pw:]
        x2 = _dot(a_r[:, :st], jnp.concatenate([stack(a_p), stack(u0)], axis=1))
        x3 = _dot(a_r[:, st:], v_st)
        yield
        r_p = r_t + x2[:, :pw]
        y0 = x2[:, pw:] + x3
        for _ in range(state_delay):
            yield

        state = states[p]
        ou = lax.dot_general(_bf(jnp.concatenate([r_p, a_p], axis=0)), _bf(state), _NT,
                             preferred_element_type=F32)
        yield
        o = ou[:c_] + y0
        u = ou[c_:] + u0
        states[p] = state * w_end + jnp.where(
            blockdiag,
            lax.dot_general(_bf(jnp.concatenate([u, xv], axis=0)),
                            jnp.concatenate([b_c, k_c], axis=0), _TN,
                            preferred_element_type=F32), 0.0)

        (sum_o,) = yield [o]
        d = o - sum_o * (1.0 / HEAD_DIM)
        (sum_dd,) = yield [d * d]
        var = sum_dd * (1.0 / HEAD_DIM)
        o = d * lax.rsqrt(var + LNX_EPS) * prow(_P_LNG) + prow(_P_LNB)
        return _bf((o + bonus) * jax.nn.silu(xz)), v_shifted

    def lora_up(c):
        r0 = pl.multiple_of(c * c_, c_)
        cur = sm_ref[pl.ds(r0, c_), :]
        prev_rows = pl.ds(pl.multiple_of(jnp.maximum(r0 - 8, 0), 8), 8)
        last = jnp.where(c > 0, sm_ref[prev_rows, :], carry_sm_scr[...])[7:8]
        prev = jnp.where(first_row_sm, last, pltpu.roll(cur, 1, 0))
        sms = cur + (prev - cur) * mus
        wl = wl_ref[...]
        lw = _dot(_bf(jnp.where(m_wlo, jnp.tanh(sms), 0.0)), wl)
        la = _dot(_bf(jnp.where(m_alo, sms, 0.0)), wl)
        lm = _dot(_bf(jnp.where(m_mv, sms, 0.0)), wl) if has_mix else None
        return lw, la, lm

    pairs = range(wb // pw)
    lane_blk = lambda x, p: None if x is None else x[:, p * pw:(p + 1) * pw]

    def body(step, carry):
        states = [state_scr[p] for p in pairs]
        chunks = [step * RWKV_CHUNKS_PER_STEP + j for j in range(RWKV_CHUNKS_PER_STEP)]
        chains = []
        for j, c in enumerate(chunks):
            lw, la, lm = lora_up(c)
            chains += [pair_chunk(p, load_pair(c, p), lane_blk(lw, p), lane_blk(la, p),
                                  lane_blk(lm, p), states, 2 * j) for p in pairs]
        results = [None] * len(chains)
        replies = [None] * len(chains)
        while any(r is None for r in results):
            requests = []
            for i, chain in enumerate(chains):
                if results[i] is not None:
                    continue
                try:
                    wanted = chain.send(replies[i])
                except StopIteration as done:
                    results[i] = done.value
                    continue
                replies[i] = None
                if wanted is not None:
                    requests.append((i, wanted))
            if requests:
                sums = segsum(jnp.concatenate([x for _, xs in requests for x in xs], axis=0))
                row = 0
                for i, xs in requests:
                    replies[i] = [sums[row + k * c_:row + (k + 1) * c_] for k in range(len(xs))]
                    row += len(xs) * c_
        for i, (y, v_shifted) in enumerate(results):
            c, p = chunks[i // len(pairs)], i % len(pairs)
            rows = pl.ds(pl.multiple_of(c * c_, c_), c_)
            lanes = slice(p * pw, (p + 1) * pw)
            y_ref[rows, lanes] = y
            if not has_mix:
                vout_ref[rows, lanes] = v_shifted
        for p in pairs:
            state_scr[p] = states[p]
        return carry

    lax.fori_loop(0, ts // (c_ * RWKV_CHUNKS_PER_STEP), body, 0)

    for i, ref in enumerate(shifted_refs):
        carry_scr[i] = ref[ts - 8:ts, :]
    carry_sm_scr[...] = sm_ref[ts - 8:ts, :]


def _rwkv_call(proj3, small3, v_first, pvec, mu_small, w_lora):
    b, s, _ = proj3.shape
    rw = pvec.shape[1]
    wb, ts = min(RWKV_BLOCK_W, rw), min(RWKV_SEQ_TILE, s)
    groups = rw // wb
    has_mix = v_first is not None
    col0 = 2 * rw // wb

    def col_spec(offset):
        return pl.BlockSpec((None, ts, wb), lambda i, g, j: (i, j, offset + g))

    in_specs = [col_spec(col0), col_spec(col0 + groups), col_spec(col0 + 2 * groups),
                col_spec(col0 + 3 * groups),
                pl.BlockSpec((None, ts, SMALL_W), lambda i, g, j: (i, j, 0))]
    args = [proj3, proj3, proj3, proj3, small3]
    if has_mix:
        in_specs.append(col_spec(0))
        args.append(v_first)
    in_specs += [
        pl.BlockSpec((_P_ROWS, wb), lambda i, g, j: (0, g)),
        pl.BlockSpec((1, SMALL_W), lambda i, g, j: (0, 0)),
        pl.BlockSpec((SMALL_W, wb), lambda i, g, j: (0, g)),
    ]
    args += [pvec, mu_small, w_lora]
    out_specs = [col_spec(0)]
    out_shape = [jax.ShapeDtypeStruct((b, s, rw), BF16)]
    if not has_mix:
        out_specs.append(col_spec(0))
        out_shape.append(jax.ShapeDtypeStruct((b, s, rw), F32))
    res = pl.pallas_call(
        functools.partial(_rwkv_kernel, has_mix),
        grid=(b, groups, s // ts),
        in_specs=in_specs,
        out_specs=out_specs,
        out_shape=out_shape,
        scratch_shapes=[
            pltpu.VMEM((wb // PAIR_W, PAIR_W, PAIR_W), F32),
            pltpu.VMEM((4, 8, wb), F32),
            pltpu.VMEM((8, SMALL_W), F32),
        ],
        compiler_params=_params("parallel", "parallel", "arbitrary"),
        name="rwkv_mixer",
    )(*args)
    return (res[0], v_first) if has_mix else (res[0], res[1])


def _merge_kernel(h_ref, ya_ref, yb_ref, wga_ref, wgb_ref, wa_ref, wb_ref, o_ref,
                  wga_scr, wgb_scr, wa_scr, wb_scr):
    @pl.when(pl.program_id(1) == 0)
    def _():
        wga_scr[...] = _bf(wga_ref[...])
        wgb_scr[...] = _bf(wgb_ref[...])
        wa_scr[...] = _bf(wa_ref[...])
        wb_scr[...] = _bf(wb_ref[...])

    h = h_ref[...]
    ga = jax.nn.sigmoid(_dot(h, wga_scr[...]))
    gb = jax.nn.sigmoid(_dot(h, wgb_scr[...]))
    pa = _dot(ya_ref[...], wa_scr[...])
    pb = _dot(yb_ref[...], wb_scr[...])
    o_ref[...] = _bf(ga * pa + gb * pb)


def _merge_call(h, ya, yb, w_in, gate_col0, w_br_a, w_br_b, layer):
    t, d = h.shape
    ka, kb = ya.shape[1], yb.shape[1]
    tm, tn = min(1024, t), 512
    nj = d // tn
    assert gate_col0 % _LANES == 0
    lane_tiles, ga0, gb0 = tn // _LANES, gate_col0 // _LANES, (gate_col0 + d) // _LANES
    return pl.pallas_call(
        _merge_kernel,
        grid=(nj, t // tm),
        in_specs=[
            pl.BlockSpec((tm, d), lambda j, i: (i, 0)),
            pl.BlockSpec((tm, ka), lambda j, i: (i, 0)),
            pl.BlockSpec((tm, kb), lambda j, i: (i, 0)),
            pl.BlockSpec((pl.Squeezed(), pl.Element(d), pl.Element(tn)),
                         lambda j, i: (layer, 0, (ga0 + j * lane_tiles) * _LANES)),
            pl.BlockSpec((pl.Squeezed(), pl.Element(d), pl.Element(tn)),
                         lambda j, i: (layer, 0, (gb0 + j * lane_tiles) * _LANES)),
            pl.BlockSpec((None, ka, tn), lambda j, i: (layer, 0, j)),
            pl.BlockSpec((None, kb, tn), lambda j, i: (layer, 0, j)),
        ],
        out_specs=pl.BlockSpec((tm, tn), lambda j, i: (i, j)),
        out_shape=jax.ShapeDtypeStruct((t, d), BF16),
        scratch_shapes=[pltpu.VMEM((d, tn), BF16), pltpu.VMEM((d, tn), BF16),
                        pltpu.VMEM((ka, tn), BF16), pltpu.VMEM((kb, tn), BF16)],
        compiler_params=_params("parallel", "arbitrary"),
        name="gated_merge",
    )(h, ya, yb, w_in, w_in, w_br_a, w_br_b)


def _out_kernel(m_ref, w_ref, x_ref, gate_ref, g_ref, o_ref, w_scr):
    @pl.when(pl.program_id(0) == 0)
    def _():
        w_scr[...] = _bf(w_ref[...])

    out = _dot(m_ref[...], w_scr[...])
    y = out * lax.rsqrt(jnp.mean(out * out, axis=-1, keepdims=True) + RMS_EPS)
    o_ref[...] = x_ref[...] + gate_ref[...] * (y * g_ref[...])


def _out_call(merged, w_out, layer, xf, gate, g_post, seq):
    t, d = xf.shape
    tm = 512
    per_seq = seq // tm
    return pl.pallas_call(
        _out_kernel,
        grid=(t // tm,),
        in_specs=[
            pl.BlockSpec((tm, d), lambda i: (i, 0)),
            pl.BlockSpec((None, d, d), lambda i: (layer, 0, 0), pipeline_mode=pl.Buffered(1)),
            pl.BlockSpec((tm, d), lambda i: (i, 0)),
            pl.BlockSpec((None, 1, d), lambda i: (i // per_seq, 0, 0)),
            pl.BlockSpec((1, d), lambda i: (0, 0)),
        ],
        out_specs=pl.BlockSpec((tm, d), lambda i: (i, 0)),
        out_shape=jax.ShapeDtypeStruct((t, d), F32),
        scratch_shapes=[pltpu.VMEM((d, d), BF16)],
        compiler_params=_params("arbitrary"),
        name="out_proj",
    )(merged, w_out, xf, gate, g_post)


def kernel(x, c, w_ada, b_ada, g_pre, w_in, w_pool, pool_scale, mu_shift, w_decay_up, w0, w_aaa_up, a0, w_mv_down, mu_mv, w_mv_up, mv0, k_k, k_a, r_k, lnx_g, lnx_b, w_br_a, w_br_b, w_out, g_post):
    bsz, seq, d = x.shape
    depth = w_ada.shape[0]
    pw = pool_scale.shape[1]
    rw = w0.shape[1]
    t = bsz * seq
    n_main = 2 * pw + 4 * rw
    n_lora = DECAY_LORA + AAA_LORA
    n_shift = 4 * rw + n_lora

    xf = x.reshape(t, d)
    c8 = jnp.pad(c, ((0, 8 - bsz), (0, 0)))
    mod = _mod_call(c8, w_ada, b_ada.reshape(depth, 1, 3 * d))

    v_first = None
    for l in range(depth):
        shift = mod[l, :bsz, :d].reshape(bsz, 1, d)
        scale = mod[l, :bsz, d:2 * d].reshape(bsz, 1, d)
        gate = mod[l, :bsz, 2 * d:].reshape(bsz, 1, d)

        small_cols = [w_in[l, :, n_main:n_main + n_lora]]
        lora_rows = [w_decay_up[l], w_aaa_up[l]]
        mu_small = [mu_shift[l, 4 * rw:n_shift]]
        mv0_l = jnp.zeros((rw,), F32)
        if l > 0:
            small_cols.append(w_mv_down[l - 1])
            lora_rows.append(w_mv_up[l - 1])
            mu_small.append(mu_mv[l - 1])
            mv0_l = mv0[l - 1]
        w_small = jnp.concatenate(small_cols, axis=1)
        w_small = jnp.pad(w_small, ((0, 0), (0, SMALL_W - w_small.shape[1])))
        w_lora = jnp.concatenate(lora_rows, axis=0)
        w_lora = _bf(jnp.pad(w_lora, ((0, SMALL_W - w_lora.shape[0]), (0, 0))))
        mu_small = jnp.concatenate(mu_small)
        mu_small = jnp.pad(mu_small, (0, SMALL_W - mu_small.shape[0])).reshape(1, SMALL_W)
        mu4 = mu_shift[l, :4 * rw].reshape(4, rw)
        pvec = jnp.concatenate(
            [mu4, jnp.stack([w0[l], a0[l], mv0_l, k_k[l], k_a[l], r_k[l].reshape(rw),
                             lnx_g[l], lnx_b[l]]),
             jnp.zeros((_P_ROWS - 12, rw), F32)], axis=0)

        h, small = _prenorm_call(xf, g_pre[l].reshape(1, d), scale, shift, w_small, seq)
        proj = _in_proj_call(h, w_in, l, n_main)
        proj3 = proj.reshape(bsz, seq, n_main)
        small3 = small.reshape(bsz, seq, SMALL_W)

        y_a = _pool_call(proj3, _bf(w_pool[l]), pool_scale[l].reshape(1, pw))
        y_b, v_first = _rwkv_call(proj3, small3, v_first, pvec, mu_small, w_lora)

        merged = _merge_call(h, y_a.reshape(t, pw), y_b.reshape(t, rw), w_in,
                             n_main + n_lora, w_br_a, w_br_b, l)
        xf = _out_call(merged, w_out, l, xf, gate, g_post[l].reshape(1, d), seq)
    return xf.reshape(bsz, seq, d)
```

```python
import functools

import jax
import jax.numpy as jnp
from jax import lax
from jax.experimental import pallas as pl
from jax.experimental.pallas import tpu as pltpu

F32 = jnp.float32
BF16 = jnp.bfloat16

RMS_EPS = 1e-6
LNX_EPS = 64e-5
POOL_WINDOWS = (2, 4, 8, 16)
HEAD_DIM = 64
DECAY_LORA = 64
AAA_LORA = 64
MV_LORA = 32

CHUNK = 64
PAIR_HEADS = 2
PAIR_W = PAIR_HEADS * HEAD_DIM
RWKV_BLOCK_W = 1024
RWKV_SEQ_TILE = 512
RWKV_CHUNKS_PER_STEP = 8
RWKV_STAGGER = 3
SMALL_W = 256

_LANES = 128
_MIB = 1024 * 1024
_V7X_VMEM_LIMIT = 56 * _MIB

_EXP_M_HALF = 0.6065306597126334
_NT = (((1,), (1,)), ((), ()))
_TN = (((0,), (0,)), ((), ()))


def _params(*sem):
    return pltpu.CompilerParams(dimension_semantics=sem, vmem_limit_bytes=_V7X_VMEM_LIMIT)


def _dot(a, b):
    return jnp.dot(a, b, preferred_element_type=F32)


def _bf(x):
    return x.astype(BF16)


def _mod_kernel(c_ref, w_ref, b_ref, o_ref):
    cond = _bf(jax.nn.silu(c_ref[...]))
    o_ref[...] = _dot(cond, _bf(w_ref[...])) + b_ref[...]


def _mod_call(c8, w_ada, b_ada3):
    depth, d, n = w_ada.shape
    tn = 512
    return pl.pallas_call(
        _mod_kernel,
        grid=(depth, n // tn),
        in_specs=[
            pl.BlockSpec((8, d), lambda l, j: (0, 0)),
            pl.BlockSpec((None, d, tn), lambda l, j: (l, 0, j)),
            pl.BlockSpec((None, 1, tn), lambda l, j: (l, 0, j)),
        ],
        out_specs=pl.BlockSpec((None, 8, tn), lambda l, j: (l, 0, j)),
        out_shape=jax.ShapeDtypeStruct((depth, 8, n), F32),
        compiler_params=_params("parallel", "parallel"),
        name="adaln_mod",
    )(c8, w_ada, b_ada3)


def _prenorm_kernel(x_ref, g_ref, sc_ref, sh_ref, ws_ref, h_ref, sm_ref):
    x = x_ref[...]
    y = x * lax.rsqrt(jnp.mean(x * x, axis=-1, keepdims=True) + RMS_EPS)
    h = _bf((y * g_ref[...]) * (1.0 + sc_ref[...]) + sh_ref[...])
    h_ref[...] = h
    sm_ref[...] = _dot(h, _bf(ws_ref[...]))


def _prenorm_call(xf, g, scale, shift, w_small, seq):
    t, d = xf.shape
    tm = 512
    per_seq = seq // tm
    return pl.pallas_call(
        _prenorm_kernel,
        grid=(t // tm,),
        in_specs=[
            pl.BlockSpec((tm, d), lambda i: (i, 0)),
            pl.BlockSpec((1, d), lambda i: (0, 0)),
            pl.BlockSpec((None, 1, d), lambda i: (i // per_seq, 0, 0)),
            pl.BlockSpec((None, 1, d), lambda i: (i // per_seq, 0, 0)),
            pl.BlockSpec((d, SMALL_W), lambda i: (0, 0)),
        ],
        out_specs=[
            pl.BlockSpec((tm, d), lambda i: (i, 0)),
            pl.BlockSpec((tm, SMALL_W), lambda i: (i, 0)),
        ],
        out_shape=[
            jax.ShapeDtypeStruct((t, d), BF16),
            jax.ShapeDtypeStruct((t, SMALL_W), F32),
        ],
        compiler_params=_params("parallel"),
        name="prenorm",
    )(xf, g, scale, shift, w_small)


def _in_proj_kernel(h_ref, w_ref, o_ref, w_scr):
    @pl.when(pl.program_id(1) == 0)
    def _():
        w_scr[...] = _bf(w_ref[...])

    o_ref[...] = _dot(h_ref[...], w_scr[...])


def _in_proj_call(h, w_in, layer, n_cols):
    m, k = h.shape
    tm, tn = min(1024, m), 1024
    return pl.pallas_call(
        _in_proj_kernel,
        grid=(n_cols // tn, m // tm),
        in_specs=[
            pl.BlockSpec((tm, k), lambda j, i: (i, 0)),
            pl.BlockSpec((None, k, tn), lambda j, i: (layer, 0, j)),
        ],
        out_specs=pl.BlockSpec((tm, tn), lambda j, i: (i, j)),
        out_shape=jax.ShapeDtypeStruct((m, n_cols), F32),
        scratch_shapes=[pltpu.VMEM((k, tn), BF16)],
        compiler_params=_params("parallel", "arbitrary"),
        name="in_proj",
    )(h, w_in)


def _pool_kernel(u_ref, z_ref, w_ref, ps_ref, y_ref):
    group = pl.program_id(1)
    u = u_ref[...]
    row = lax.broadcasted_iota(jnp.int32, (u.shape[0], 1), 0)

    def shifted(a, s):
        return jnp.where(row >= s, pltpu.roll(a, s, 0), 0.0)

    for gi, window in enumerate(POOL_WINDOWS):

        @pl.when(group == gi)
        def _():
            acc, span = u, 1
            while span < window:
                acc = acc + shifted(acc, span)
                span *= 2
            cnt = jnp.minimum(row + 1, window).astype(F32)
            pooled = acc / cnt - u
            y = _dot(_bf(pooled), w_ref[...])
            y_ref[...] = _bf(y * ps_ref[...] * jax.nn.silu(z_ref[...]))


def _pool_call(proj3, w_pool, pool_scale):
    b, s, _ = proj3.shape
    groups, gd, _ = w_pool.shape
    return pl.pallas_call(
        _pool_kernel,
        grid=(b, groups),
        in_specs=[
            pl.BlockSpec((None, s, gd), lambda i, g: (i, 0, g)),
            pl.BlockSpec((None, s, gd), lambda i, g: (i, 0, groups + g)),
            pl.BlockSpec((None, gd, gd), lambda i, g: (g, 0, 0)),
            pl.BlockSpec((1, gd), lambda i, g: (0, g)),
        ],
        out_specs=pl.BlockSpec((None, s, gd), lambda i, g: (i, 0, g)),
        out_shape=jax.ShapeDtypeStruct((b, s, groups * gd), BF16),
        compiler_params=_params("parallel", "parallel"),
        name="pool_mixer",
    )(proj3, proj3, w_pool, pool_scale)


(_P_MU_R, _P_MU_K, _P_MU_V, _P_MU_Z, _P_W0, _P_A0, _P_MV0, _P_KK, _P_KA, _P_RK,
 _P_LNG, _P_LNB) = range(12)
_P_ROWS = 16


def _split_terms(x, terms):
    pieces, rem = [], x
    for _ in range(terms):
        p = _bf(rem)
        pieces.append(p)
        rem = rem - p.astype(F32)
    return pieces


def _rwkv_kernel(has_mix, *refs):
    if has_mix:
        (r_ref, k_ref, v_ref, z_ref, sm_ref, vf_ref, pv_ref, mus_ref, wl_ref,
         y_ref, state_scr, carry_scr, carry_sm_scr) = refs
    else:
        (r_ref, k_ref, v_ref, z_ref, sm_ref, pv_ref, mus_ref, wl_ref,
         y_ref, vout_ref, state_scr, carry_scr, carry_sm_scr) = refs
    ts, wb = r_ref.shape
    c_, pw, st = CHUNK, PAIR_W, PAIR_HEADS * CHUNK
    shifted_refs = (r_ref, k_ref, v_ref, z_ref)

    @pl.when(pl.program_id(2) == 0)
    def _():
        state_scr[...] = jnp.zeros_like(state_scr)
        carry_scr[...] = jnp.zeros_like(carry_scr)
        carry_sm_scr[...] = jnp.zeros_like(carry_sm_scr)

    pv = pv_ref[...]
    mus = mus_ref[...]

    def iota(shape, dim):
        return lax.broadcasted_iota(jnp.int32, shape, dim)

    first_row = iota((c_, pw), 0) == 0
    first_row_sm = iota((c_, SMALL_W), 0) == 0
    lane_sm = iota((c_, SMALL_W), 1)
    m_wlo = lane_sm < DECAY_LORA
    m_alo = (lane_sm >= DECAY_LORA) & (lane_sm < DECAY_LORA + AAA_LORA)
    m_mv = (lane_sm >= DECAY_LORA + AAA_LORA) & (lane_sm < DECAY_LORA + AAA_LORA + MV_LORA)
    lane = iota((c_, pw), 1)
    head_masks = [lane // HEAD_DIM == j for j in range(PAIR_HEADS)]
    sq_r, sq_c = iota((st, pw), 0), iota((st, pw), 1)
    blockdiag = sq_r // HEAD_DIM == sq_c // HEAD_DIM
    ones_bd = jnp.where(blockdiag, 1.0, 0.0).astype(BF16)
    ones_bd2 = jnp.concatenate([ones_bd, ones_bd], axis=0)
    tok_r, tok_c = iota((c_, 2 * st), 0), iota((c_, 2 * st), 1) % c_
    strict_lower = (tok_r > tok_c)[:, :st]
    lower = tok_r >= tok_c
    eye = jnp.where(tok_r == tok_c, 1.0, 0.0).astype(F32)[:, :st]
    tri = jnp.where(iota((c_, c_), 0) >= iota((c_, c_), 1), 1.0, 0.0).astype(BF16)
    tri2 = jnp.concatenate([tri, tri], axis=1)

    def segsum(x):
        return _dot(jnp.concatenate(_split_terms(x, 2), axis=1), ones_bd2)

    def stack(x):
        return _bf(jnp.concatenate([jnp.where(m, x, 0.0) for m in head_masks], axis=0))

    def blockdiag_of(x):
        return _bf(jnp.where(blockdiag, jnp.concatenate([x] * PAIR_HEADS, axis=0), 0.0))

    def load_pair(c, p):
        lanes = slice(p * pw, (p + 1) * pw)
        r0 = pl.multiple_of(c * c_, c_)
        rows = pl.ds(r0, c_)
        prev_rows = pl.ds(pl.multiple_of(jnp.maximum(r0 - 8, 0), 8), 8)
        cur = [ref[rows, lanes] for ref in shifted_refs]
        last = [jnp.where(c > 0, ref[prev_rows, lanes], carry_scr[i, :, lanes])[7:8]
                for i, ref in enumerate(shifted_refs)]
        vf = vf_ref[rows, lanes] if has_mix else None
        return cur, last, vf

    def pair_chunk(p, loaded, lw, la, lm, states, state_delay):
        lanes = slice(p * pw, (p + 1) * pw)
        cur, last, vf = loaded
        prow = lambda i: pv[i:i + 1, lanes]

        def tshift(i, mu):
            prev = jnp.where(first_row, last[i], pltpu.roll(cur[i], 1, 0))
            return cur[i] + (prev - cur[i]) * mu

        xr = tshift(0, prow(_P_MU_R))
        xk = tshift(1, prow(_P_MU_K))
        xv = tshift(2, prow(_P_MU_V))
        xz = tshift(3, prow(_P_MU_Z))
        v_shifted = xv

        ld = -_EXP_M_HALF * jax.nn.sigmoid(prow(_P_W0) + lw)
        asig = jax.nn.sigmoid(prow(_P_A0) + la)
        if has_mix:
            xv = xv + (vf - xv) * jax.nn.sigmoid(prow(_P_MV0) + lm)
        kk = xk * prow(_P_KK)
        kp = xk * (1.0 + (asig - 1.0) * prow(_P_KA))
        cum = _dot(tri2, jnp.concatenate(_split_terms(ld, 2), axis=0))
        kk_sq, rk_sum = yield [kk * kk, xr * kp * prow(_P_RK)]
        kkn = kk * lax.rsqrt(jnp.maximum(kk_sq, 1e-24))
        a = -kkn
        b = kkn * asig
        bonus = rk_sum * xv
        e_in = jnp.exp(cum)
        e_out = jnp.exp(-cum)
        w_end = e_in[c_ - 1:c_, :]
        e_end = w_end * e_out
        a_t = a * jnp.exp(cum - ld)
        r_t = xr * e_in
        b_t = b * e_out
        k_t = kp * e_out
        b_c = _bf(b * e_end)
        k_c = _bf(kp * e_end)
        v_st = stack(xv)

        g = lax.dot_general(_bf(jnp.concatenate([a_t, r_t], axis=0)),
                            jnp.concatenate([stack(b_t), stack(k_t)], axis=0), _NT,
                            preferred_element_type=F32)
        a_ab = jnp.where(strict_lower, g[:c_, :st], 0.0)
        a_ak = jnp.where(strict_lower, g[:c_, st:], 0.0)
        a_r = _bf(jnp.where(lower, g[c_:, :], 0.0))
        yield

        akv = _dot(_bf(a_ak), v_st)
        t_inv = eye + a_ab
        a_pow = _dot(_bf(a_ab), blockdiag_of(a_ab))
        yield
        span = 2
        while span < c_:
            pow_bd = blockdiag_of(a_pow)
            span *= 2
            if span < c_:
                both = _dot(_bf(jnp.concatenate([t_inv, a_pow], axis=0)), pow_bd)
                t_inv, a_pow = t_inv + both[:c_], both[c_:]
            else:
                t_inv = t_inv + _dot(_bf(t_inv), pow_bd)
            yield

        x1 = _dot(_bf(t_inv), jnp.concatenate([stack(a_t), stack(akv)], axis=1))
        yield
        a_p, u0 = x1[:, :pw], x1[:, pw:]
        x2 = _dot(a_r[:, :st], jnp.concatenate([stack(a_p), stack(u0)], axis=1))
        x3 = _dot(a_r[:, st:], v_st)
        yield
        r_p = r_t + x2[:, :pw]
        y0 = x2[:, pw:] + x3
        for _ in range(state_delay):
            yield

        state = states[p]
        ou = lax.dot_general(_bf(jnp.concatenate([r_p, a_p], axis=0)), _bf(state), _NT,
                             preferred_element_type=F32)
        yield
        o = ou[:c_] + y0
        u = ou[c_:] + u0
        states[p] = state * w_end + jnp.where(
            blockdiag,
            lax.dot_general(_bf(jnp.concatenate([u, xv], axis=0)),
                            jnp.concatenate([b_c, k_c], axis=0), _TN,
                            preferred_element_type=F32), 0.0)

        (sum_o,) = yield [o]
        d = o - sum_o * (1.0 / HEAD_DIM)
        (sum_dd,) = yield [d * d]
        var = sum_dd * (1.0 / HEAD_DIM)
        o = d * lax.rsqrt(var + LNX_EPS) * prow(_P_LNG) + prow(_P_LNB)
        return _bf((o + bonus) * jax.nn.silu(xz)), v_shifted

    def lora_up(c):
        r0 = pl.multiple_of(c * c_, c_)
        cur = sm_ref[pl.ds(r0, c_), :]
        prev_rows = pl.ds(pl.multiple_of(jnp.maximum(r0 - 8, 0), 8), 8)
        last = jnp.where(c > 0, sm_ref[prev_rows, :], carry_sm_scr[...])[7:8]
        prev = jnp.where(first_row_sm, last, pltpu.roll(cur, 1, 0))
        sms = cur + (prev - cur) * mus
        wl = wl_ref[...]
        lw = _dot(_bf(jnp.where(m_wlo, jnp.tanh(sms), 0.0)), wl)
        la = _dot(_bf(jnp.where(m_alo, sms, 0.0)), wl)
        lm = _dot(_bf(jnp.where(m_mv, sms, 0.0)), wl) if has_mix else None
        return lw, la, lm

    pairs = range(wb // pw)
    lane_blk = lambda x, p: None if x is None else x[:, p * pw:(p + 1) * pw]

    def body(step, carry):
        states = [state_scr[p] for p in pairs]
        chunks = [step * RWKV_CHUNKS_PER_STEP + j for j in range(RWKV_CHUNKS_PER_STEP)]
        chains = []
        for j, c in enumerate(chunks):
            lw, la, lm = lora_up(c)
            chains += [pair_chunk(p, load_pair(c, p), lane_blk(lw, p), lane_blk(la, p),
                                  lane_blk(lm, p), states,
                                  max(0, 2 - RWKV_STAGGER) * j) for p in pairs]
        first_round = [(i // len(pairs)) * RWKV_STAGGER for i in range(len(chains))]
        results = [None] * len(chains)
        replies = [None] * len(chains)
        round_idx = -1
        while any(r is None for r in results):
            round_idx += 1
            requests = []
            for i, chain in enumerate(chains):
                if results[i] is not None or round_idx < first_round[i]:
                    continue
                try:
                    wanted = chain.send(replies[i])
                except StopIteration as done:
                    results[i] = done.value
                    continue
                replies[i] = None
                if wanted is not None:
                    requests.append((i, wanted))
            if requests:
                sums = segsum(jnp.concatenate([x for _, xs in requests for x in xs], axis=0))
                row = 0
                for i, xs in requests:
                    replies[i] = [sums[row + k * c_:row + (k + 1) * c_] for k in range(len(xs))]
                    row += len(xs) * c_
        for i, (y, v_shifted) in enumerate(results):
            c, p = chunks[i // len(pairs)], i % len(pairs)
            rows = pl.ds(pl.multiple_of(c * c_, c_), c_)
            lanes = slice(p * pw, (p + 1) * pw)
            y_ref[rows, lanes] = y
            if not has_mix:
                vout_ref[rows, lanes] = v_shifted
        for p in pairs:
            state_scr[p] = states[p]
        return carry

    lax.fori_loop(0, ts // (c_ * RWKV_CHUNKS_PER_STEP), body, 0)

    for i, ref in enumerate(shifted_refs):
        carry_scr[i] = ref[ts - 8:ts, :]
    carry_sm_scr[...] = sm_ref[ts - 8:ts, :]


def _rwkv_call(proj3, small3, v_first, pvec, mu_small, w_lora):
    b, s, _ = proj3.shape
    rw = pvec.shape[1]
    wb, ts = min(RWKV_BLOCK_W, rw), min(RWKV_SEQ_TILE, s)
    groups = rw // wb
    has_mix = v_first is not None
    col0 = 2 * rw // wb

    def col_spec(offset):
        return pl.BlockSpec((None, ts, wb), lambda i, g, j: (i, j, offset + g))

    in_specs = [col_spec(col0), col_spec(col0 + groups), col_spec(col0 + 2 * groups),
                col_spec(col0 + 3 * groups),
                pl.BlockSpec((None, ts, SMALL_W), lambda i, g, j: (i, j, 0))]
    args = [proj3, proj3, proj3, proj3, small3]
    if has_mix:
        in_specs.append(col_spec(0))
        args.append(v_first)
    in_specs += [
        pl.BlockSpec((_P_ROWS, wb), lambda i, g, j: (0, g)),
        pl.BlockSpec((1, SMALL_W), lambda i, g, j: (0, 0)),
        pl.BlockSpec((SMALL_W, wb), lambda i, g, j: (0, g)),
    ]
    args += [pvec, mu_small, w_lora]
    out_specs = [col_spec(0)]
    out_shape = [jax.ShapeDtypeStruct((b, s, rw), BF16)]
    if not has_mix:
        out_specs.append(col_spec(0))
        out_shape.append(jax.ShapeDtypeStruct((b, s, rw), F32))
    res = pl.pallas_call(
        functools.partial(_rwkv_kernel, has_mix),
        grid=(b, groups, s // ts),
        in_specs=in_specs,
        out_specs=out_specs,
        out_shape=out_shape,
        scratch_shapes=[
            pltpu.VMEM((wb // PAIR_W, PAIR_W, PAIR_W), F32),
            pltpu.VMEM((4, 8, wb), F32),
            pltpu.VMEM((8, SMALL_W), F32),
        ],
        compiler_params=_params("parallel", "parallel", "arbitrary"),
        name="rwkv_mixer",
    )(*args)
    return (res[0], v_first) if has_mix else (res[0], res[1])


def _merge_kernel(h_ref, ya_ref, yb_ref, wga_ref, wgb_ref, wa_ref, wb_ref, o_ref,
                  wga_scr, wgb_scr, wa_scr, wb_scr):
    @pl.when(pl.program_id(1) == 0)
    def _():
        wga_scr[...] = _bf(wga_ref[...])
        wgb_scr[...] = _bf(wgb_ref[...])
        wa_scr[...] = _bf(wa_ref[...])
        wb_scr[...] = _bf(wb_ref[...])

    h = h_ref[...]
    ga = jax.nn.sigmoid(_dot(h, wga_scr[...]))
    gb = jax.nn.sigmoid(_dot(h, wgb_scr[...]))
    pa = _dot(ya_ref[...], wa_scr[...])
    pb = _dot(yb_ref[...], wb_scr[...])
    o_ref[...] = _bf(ga * pa + gb * pb)


def _merge_call(h, ya, yb, w_in, gate_col0, w_br_a, w_br_b, layer):
    t, d = h.shape
    ka, kb = ya.shape[1], yb.shape[1]
    tm, tn = min(1024, t), 512
    nj = d // tn
    assert gate_col0 % _LANES == 0
    lane_tiles, ga0, gb0 = tn // _LANES, gate_col0 // _LANES, (gate_col0 + d) // _LANES
    return pl.pallas_call(
        _merge_kernel,
        grid=(nj, t // tm),
        in_specs=[
            pl.BlockSpec((tm, d), lambda j, i: (i, 0)),
            pl.BlockSpec((tm, ka), lambda j, i: (i, 0)),
            pl.BlockSpec((tm, kb), lambda j, i: (i, 0)),
            pl.BlockSpec((pl.Squeezed(), pl.Element(d), pl.Element(tn)),
                         lambda j, i: (layer, 0, (ga0 + j * lane_tiles) * _LANES)),
            pl.BlockSpec((pl.Squeezed(), pl.Element(d), pl.Element(tn)),
                         lambda j, i: (layer, 0, (gb0 + j * lane_tiles) * _LANES)),
            pl.BlockSpec((None, ka, tn), lambda j, i: (layer, 0, j)),
            pl.BlockSpec((None, kb, tn), lambda j, i: (layer, 0, j)),
        ],
        out_specs=pl.BlockSpec((tm, tn), lambda j, i: (i, j)),
        out_shape=jax.ShapeDtypeStruct((t, d), BF16),
        scratch_shapes=[pltpu.VMEM((d, tn), BF16), pltpu.VMEM((d, tn), BF16),
                        pltpu.VMEM((ka, tn), BF16), pltpu.VMEM((kb, tn), BF16)],
        compiler_params=_params("parallel", "arbitrary"),
        name="gated_merge",
    )(h, ya, yb, w_in, w_in, w_br_a, w_br_b)


def _out_kernel(m_ref, w_ref, x_ref, gate_ref, g_ref, o_ref, w_scr):
    @pl.when(pl.program_id(0) == 0)
    def _():
        w_scr[...] = _bf(w_ref[...])

    out = _dot(m_ref[...], w_scr[...])
    y = out * lax.rsqrt(jnp.mean(out * out, axis=-1, keepdims=True) + RMS_EPS)
    o_ref[...] = x_ref[...] + gate_ref[...] * (y * g_ref[...])


def _out_call(merged, w_out, layer, xf, gate, g_post, seq):
    t, d = xf.shape
    tm = 512
    per_seq = seq // tm
    return pl.pallas_call(
        _out_kernel,
        grid=(t // tm,),
        in_specs=[
            pl.BlockSpec((tm, d), lambda i: (i, 0)),
            pl.BlockSpec((None, d, d), lambda i: (layer, 0, 0), pipeline_mode=pl.Buffered(1)),
            pl.BlockSpec((tm, d), lambda i: (i, 0)),
            pl.BlockSpec((None, 1, d), lambda i: (i // per_seq, 0, 0)),
            pl.BlockSpec((1, d), lambda i: (0, 0)),
        ],
        out_specs=pl.BlockSpec((tm, d), lambda i: (i, 0)),
        out_shape=jax.ShapeDtypeStruct((t, d), F32),
        scratch_shapes=[pltpu.VMEM((d, d), BF16)],
        compiler_params=_params("arbitrary"),
        name="out_proj",
    )(merged, w_out, xf, gate, g_post)


def kernel(x, c, w_ada, b_ada, g_pre, w_in, w_pool, pool_scale, mu_shift, w_decay_up, w0, w_aaa_up, a0, w_mv_down, mu_mv, w_mv_up, mv0, k_k, k_a, r_k, lnx_g, lnx_b, w_br_a, w_br_b, w_out, g_post):
    bsz, seq, d = x.shape
    depth = w_ada.shape[0]
    pw = pool_scale.shape[1]
    rw = w0.shape[1]
    t = bsz * seq
    n_main = 2 * pw + 4 * rw
    n_lora = DECAY_LORA + AAA_LORA
    n_shift = 4 * rw + n_lora

    xf = x.reshape(t, d)
    c8 = jnp.pad(c, ((0, 8 - bsz), (0, 0)))
    mod = _mod_call(c8, w_ada, b_ada.reshape(depth, 1, 3 * d))

    v_first = None
    for l in range(depth):
        shift = mod[l, :bsz, :d].reshape(bsz, 1, d)
        scale = mod[l, :bsz, d:2 * d].reshape(bsz, 1, d)
        gate = mod[l, :bsz, 2 * d:].reshape(bsz, 1, d)

        small_cols = [w_in[l, :, n_main:n_main + n_lora]]
        lora_rows = [w_decay_up[l], w_aaa_up[l]]
        mu_small = [mu_shift[l, 4 * rw:n_shift]]
        mv0_l = jnp.zeros((rw,), F32)
        if l > 0:
            small_cols.append(w_mv_down[l - 1])
            lora_rows.append(w_mv_up[l - 1])
            mu_small.append(mu_mv[l - 1])
            mv0_l = mv0[l - 1]
        w_small = jnp.concatenate(small_cols, axis=1)
        w_small = jnp.pad(w_small, ((0, 0), (0, SMALL_W - w_small.shape[1])))
        w_lora = jnp.concatenate(lora_rows, axis=0)
        w_lora = _bf(jnp.pad(w_lora, ((0, SMALL_W - w_lora.shape[0]), (0, 0))))
        mu_small = jnp.concatenate(mu_small)
        mu_small = jnp.pad(mu_small, (0, SMALL_W - mu_small.shape[0])).reshape(1, SMALL_W)
        mu4 = mu_shift[l, :4 * rw].reshape(4, rw)
        pvec = jnp.concatenate(
            [mu4, jnp.stack([w0[l], a0[l], mv0_l, k_k[l], k_a[l], r_k[l].reshape(rw),
                             lnx_g[l], lnx_b[l]]),
             jnp.zeros((_P_ROWS - 12, rw), F32)], axis=0)

        h, small = _prenorm_call(xf, g_pre[l].reshape(1, d), scale, shift, w_small, seq)
        proj = _in_proj_call(h, w_in, l, n_main)
        proj3 = proj.reshape(bsz, seq, n_main)
        small3 = small.reshape(bsz, seq, SMALL_W)

        y_a = _pool_call(proj3, _bf(w_pool[l]), pool_scale[l].reshape(1, pw))
        y_b, v_first = _rwkv_call(proj3, small3, v_first, pvec, mu_small, w_lora)

        merged = _merge_call(h, y_a.reshape(t, pw), y_b.reshape(t, rw), w_in,
                             n_main + n_lora, w_br_a, w_br_b, l)
        xf = _out_call(merged, w_out, l, xf, gate, g_post[l].reshape(1, d), seq)
    return xf.reshape(bsz, seq, d)
```

```python
import functools

import jax
import jax.numpy as jnp
from jax import lax
from jax.experimental import pallas as pl
from jax.experimental.pallas import tpu as pltpu

F32 = jnp.float32
BF16 = jnp.bfloat16

RMS_EPS = 1e-6
LNX_EPS = 64e-5
POOL_WINDOWS = (2, 4, 8, 16)
HEAD_DIM = 64
DECAY_LORA = 64
AAA_LORA = 64
MV_LORA = 32

CHUNK = 64
PAIR_HEADS = 2
PAIR_W = PAIR_HEADS * HEAD_DIM
RWKV_BLOCK_W = 1024
RWKV_SEQ_TILE = 512
RWKV_CHUNKS_PER_STEP = 8
RWKV_STAGGER = 5
SMALL_W = 256

_LANES = 128
_MIB = 1024 * 1024
_V7X_VMEM_LIMIT = 56 * _MIB

_EXP_M_HALF = 0.6065306597126334
_NT = (((1,), (1,)), ((), ()))
_TN = (((0,), (0,)), ((), ()))


def _params(*sem):
    return pltpu.CompilerParams(dimension_semantics=sem, vmem_limit_bytes=_V7X_VMEM_LIMIT)


def _dot(a, b):
    return jnp.dot(a, b, preferred_element_type=F32)


def _bf(x):
    return x.astype(BF16)


def _mod_kernel(c_ref, w_ref, b_ref, o_ref):
    cond = _bf(jax.nn.silu(c_ref[...]))
    o_ref[...] = _dot(cond, _bf(w_ref[...])) + b_ref[...]


def _mod_call(c8, w_ada, b_ada3):
    depth, d, n = w_ada.shape
    tn = 1024
    return pl.pallas_call(
        _mod_kernel,
        grid=(depth, n // tn),
        in_specs=[
            pl.BlockSpec((8, d), lambda l, j: (0, 0)),
            pl.BlockSpec((None, d, tn), lambda l, j: (l, 0, j)),
            pl.BlockSpec((None, 1, tn), lambda l, j: (l, 0, j)),
        ],
        out_specs=pl.BlockSpec((None, 8, tn), lambda l, j: (l, 0, j)),
        out_shape=jax.ShapeDtypeStruct((depth, 8, n), F32),
        compiler_params=_params("parallel", "parallel"),
        name="adaln_mod",
    )(c8, w_ada, b_ada3)


def _prenorm_kernel(x_ref, g_ref, sc_ref, sh_ref, ws_ref, h_ref, sm_ref):
    x = x_ref[...]
    y = x * lax.rsqrt(jnp.mean(x * x, axis=-1, keepdims=True) + RMS_EPS)
    h = _bf((y * g_ref[...]) * (1.0 + sc_ref[...]) + sh_ref[...])
    h_ref[...] = h
    sm_ref[...] = _dot(h, _bf(ws_ref[...]))


def _prenorm_call(xf, g, scale, shift, w_small, seq):
    t, d = xf.shape
    tm = 1024
    per_seq = seq // tm
    return pl.pallas_call(
        _prenorm_kernel,
        grid=(t // tm,),
        in_specs=[
            pl.BlockSpec((tm, d), lambda i: (i, 0)),
            pl.BlockSpec((1, d), lambda i: (0, 0)),
            pl.BlockSpec((None, 1, d), lambda i: (i // per_seq, 0, 0)),
            pl.BlockSpec((None, 1, d), lambda i: (i // per_seq, 0, 0)),
            pl.BlockSpec((d, SMALL_W), lambda i: (0, 0)),
        ],
        out_specs=[
            pl.BlockSpec((tm, d), lambda i: (i, 0)),
            pl.BlockSpec((tm, SMALL_W), lambda i: (i, 0)),
        ],
        out_shape=[
            jax.ShapeDtypeStruct((t, d), BF16),
            jax.ShapeDtypeStruct((t, SMALL_W), F32),
        ],
        compiler_params=_params("parallel"),
        name="prenorm",
    )(xf, g, scale, shift, w_small)


def _in_proj_kernel(h_ref, w_ref, o_ref, w_scr):
    @pl.when(pl.program_id(1) == 0)
    def _():
        w_scr[...] = _bf(w_ref[...])

    o_ref[...] = _dot(h_ref[...], w_scr[...])


def _in_proj_call(h, w_in, layer, n_cols):
    m, k = h.shape
    tm, tn = min(1024, m), 1024
    return pl.pallas_call(
        _in_proj_kernel,
        grid=(n_cols // tn, m // tm),
        in_specs=[
            pl.BlockSpec((tm, k), lambda j, i: (i, 0)),
            pl.BlockSpec((None, k, tn), lambda j, i: (layer, 0, j)),
        ],
        out_specs=pl.BlockSpec((tm, tn), lambda j, i: (i, j)),
        out_shape=jax.ShapeDtypeStruct((m, n_cols), F32),
        scratch_shapes=[pltpu.VMEM((k, tn), BF16)],
        compiler_params=_params("parallel", "arbitrary"),
        name="in_proj",
    )(h, w_in)


def _pool_kernel(u_ref, z_ref, w_ref, ps_ref, y_ref):
    group = pl.program_id(1)
    u = u_ref[...]
    row = lax.broadcasted_iota(jnp.int32, (u.shape[0], 1), 0)

    def shifted(a, s):
        return jnp.where(row >= s, pltpu.roll(a, s, 0), 0.0)

    for gi, window in enumerate(POOL_WINDOWS):

        @pl.when(group == gi)
        def _():
            acc, span = u, 1
            while span < window:
                acc = acc + shifted(acc, span)
                span *= 2
            cnt = jnp.minimum(row + 1, window).astype(F32)
            pooled = acc / cnt - u
            y = _dot(_bf(pooled), w_ref[...])
            y_ref[...] = _bf(y * ps_ref[...] * jax.nn.silu(z_ref[...]))


def _pool_call(proj3, w_pool, pool_scale):
    b, s, _ = proj3.shape
    groups, gd, _ = w_pool.shape
    return pl.pallas_call(
        _pool_kernel,
        grid=(b, groups),
        in_specs=[
            pl.BlockSpec((None, s, gd), lambda i, g: (i, 0, g)),
            pl.BlockSpec((None, s, gd), lambda i, g: (i, 0, groups + g)),
            pl.BlockSpec((None, gd, gd), lambda i, g: (g, 0, 0)),
            pl.BlockSpec((1, gd), lambda i, g: (0, g)),
        ],
        out_specs=pl.BlockSpec((None, s, gd), lambda i, g: (i, 0, g)),
        out_shape=jax.ShapeDtypeStruct((b, s, groups * gd), BF16),
        compiler_params=_params("parallel", "parallel"),
        name="pool_mixer",
    )(proj3, proj3, w_pool, pool_scale)


(_P_MU_R, _P_MU_K, _P_MU_V, _P_MU_Z, _P_W0, _P_A0, _P_MV0, _P_KK, _P_KA, _P_RK,
 _P_LNG, _P_LNB) = range(12)
_P_ROWS = 16


def _split_terms(x, terms):
    pieces, rem = [], x
    for _ in range(terms):
        p = _bf(rem)
        pieces.append(p)
        rem = rem - p.astype(F32)
    return pieces


def _rwkv_kernel(has_mix, *refs):
    if has_mix:
        (r_ref, k_ref, v_ref, z_ref, sm_ref, vf_ref, pv_ref, mus_ref, wl_ref,
         y_ref, state_scr, carry_scr, carry_sm_scr) = refs
    else:
        (r_ref, k_ref, v_ref, z_ref, sm_ref, pv_ref, mus_ref, wl_ref,
         y_ref, vout_ref, state_scr, carry_scr, carry_sm_scr) = refs
    ts, wb = r_ref.shape
    c_, pw, st = CHUNK, PAIR_W, PAIR_HEADS * CHUNK
    shifted_refs = (r_ref, k_ref, v_ref, z_ref)

    @pl.when(pl.program_id(2) == 0)
    def _():
        state_scr[...] = jnp.zeros_like(state_scr)
        carry_scr[...] = jnp.zeros_like(carry_scr)
        carry_sm_scr[...] = jnp.zeros_like(carry_sm_scr)

    pv = pv_ref[...]
    mus = mus_ref[...]

    def iota(shape, dim):
        return lax.broadcasted_iota(jnp.int32, shape, dim)

    first_row = iota((c_, pw), 0) == 0
    first_row_sm = iota((c_, SMALL_W), 0) == 0
    lane_sm = iota((c_, SMALL_W), 1)
    m_wlo = lane_sm < DECAY_LORA
    m_alo = (lane_sm >= DECAY_LORA) & (lane_sm < DECAY_LORA + AAA_LORA)
    m_mv = (lane_sm >= DECAY_LORA + AAA_LORA) & (lane_sm < DECAY_LORA + AAA_LORA + MV_LORA)
    lane = iota((c_, pw), 1)
    head_masks = [lane // HEAD_DIM == j for j in range(PAIR_HEADS)]
    sq_r, sq_c = iota((st, pw), 0), iota((st, pw), 1)
    blockdiag = sq_r // HEAD_DIM == sq_c // HEAD_DIM
    ones_bd = jnp.where(blockdiag, 1.0, 0.0).astype(BF16)
    ones_bd2 = jnp.concatenate([ones_bd, ones_bd], axis=0)
    tok_r, tok_c = iota((c_, 2 * st), 0), iota((c_, 2 * st), 1) % c_
    strict_lower = (tok_r > tok_c)[:, :st]
    lower = tok_r >= tok_c
    eye = jnp.where(tok_r == tok_c, 1.0, 0.0).astype(F32)[:, :st]
    tri = jnp.where(iota((c_, c_), 0) >= iota((c_, c_), 1), 1.0, 0.0).astype(BF16)
    tri2 = jnp.concatenate([tri, tri], axis=1)

    def segsum(x):
        return _dot(jnp.concatenate(_split_terms(x, 2), axis=1), ones_bd2)

    def stack(x):
        return _bf(jnp.concatenate([jnp.where(m, x, 0.0) for m in head_masks], axis=0))

    def blockdiag_of(x):
        return _bf(jnp.where(blockdiag, jnp.concatenate([x] * PAIR_HEADS, axis=0), 0.0))

    def load_pair(c, p):
        lanes = slice(p * pw, (p + 1) * pw)
        r0 = pl.multiple_of(c * c_, c_)
        rows = pl.ds(r0, c_)
        prev_rows = pl.ds(pl.multiple_of(jnp.maximum(r0 - 8, 0), 8), 8)
        cur = [ref[rows, lanes] for ref in shifted_refs]
        last = [jnp.where(c > 0, ref[prev_rows, lanes], carry_scr[i, :, lanes])[7:8]
                for i, ref in enumerate(shifted_refs)]
        vf = vf_ref[rows, lanes] if has_mix else None
        return cur, last, vf

    def pair_chunk(p, loaded, lw, la, lm, states, state_delay):
        lanes = slice(p * pw, (p + 1) * pw)
        cur, last, vf = loaded
        prow = lambda i: pv[i:i + 1, lanes]

        def tshift(i, mu):
            prev = jnp.where(first_row, last[i], pltpu.roll(cur[i], 1, 0))
            return cur[i] + (prev - cur[i]) * mu

        xr = tshift(0, prow(_P_MU_R))
        xk = tshift(1, prow(_P_MU_K))
        xv = tshift(2, prow(_P_MU_V))
        xz = tshift(3, prow(_P_MU_Z))
        v_shifted = xv

        ld = -_EXP_M_HALF * jax.nn.sigmoid(prow(_P_W0) + lw)
        asig = jax.nn.sigmoid(prow(_P_A0) + la)
        if has_mix:
            xv = xv + (vf - xv) * jax.nn.sigmoid(prow(_P_MV0) + lm)
        kk = xk * prow(_P_KK)
        kp = xk * (1.0 + (asig - 1.0) * prow(_P_KA))
        cum = _dot(tri2, jnp.concatenate(_split_terms(ld, 2), axis=0))
        kk_sq, rk_sum = yield [kk * kk, xr * kp * prow(_P_RK)]
        kkn = kk * lax.rsqrt(jnp.maximum(kk_sq, 1e-24))
        a = -kkn
        b = kkn * asig
        bonus = rk_sum * xv
        e_in = jnp.exp(cum)
        e_out = jnp.exp(-cum)
        w_end = e_in[c_ - 1:c_, :]
        e_end = w_end * e_out
        a_t = a * jnp.exp(cum - ld)
        r_t = xr * e_in
        b_t = b * e_out
        k_t = kp * e_out
        b_c = _bf(b * e_end)
        k_c = _bf(kp * e_end)
        v_st = stack(xv)

        g = lax.dot_general(_bf(jnp.concatenate([a_t, r_t], axis=0)),
                            jnp.concatenate([stack(b_t), stack(k_t)], axis=0), _NT,
                            preferred_element_type=F32)
        a_ab = jnp.where(strict_lower, g[:c_, :st], 0.0)
        a_ak = jnp.where(strict_lower, g[:c_, st:], 0.0)
        a_r = _bf(jnp.where(lower, g[c_:, :], 0.0))
        yield

        akv = _dot(_bf(a_ak), v_st)
        t_inv = eye + a_ab
        a_pow = _dot(_bf(a_ab), blockdiag_of(a_ab))
        yield
        span = 2
        while span < c_:
            pow_bd = blockdiag_of(a_pow)
            span *= 2
            if span < c_:
                both = _dot(_bf(jnp.concatenate([t_inv, a_pow], axis=0)), pow_bd)
                t_inv, a_pow = t_inv + both[:c_], both[c_:]
            else:
                t_inv = t_inv + _dot(_bf(t_inv), pow_bd)
            yield

        x1 = _dot(_bf(t_inv), jnp.concatenate([stack(a_t), stack(akv)], axis=1))
        yield
        a_p, u0 = x1[:, :pw], x1[:, pw:]
        x2 = _dot(a_r[:, :st], jnp.concatenate([stack(a_p), stack(u0)], axis=1))
        x3 = _dot(a_r[:, st:], v_st)
        yield
        r_p = r_t + x2[:, :pw]
        y0 = x2[:, pw:] + x3
        for _ in range(state_delay):
            yield

        state = states[p]
        ou = lax.dot_general(_bf(jnp.concatenate([r_p, a_p], axis=0)), _bf(state), _NT,
                             preferred_element_type=F32)
        yield
        o = ou[:c_] + y0
        u = ou[c_:] + u0
        states[p] = state * w_end + jnp.where(
            blockdiag,
            lax.dot_general(_bf(jnp.concatenate([u, xv], axis=0)),
                            jnp.concatenate([b_c, k_c], axis=0), _TN,
                            preferred_element_type=F32), 0.0)

        (sum_o,) = yield [o]
        d = o - sum_o * (1.0 / HEAD_DIM)
        (sum_dd,) = yield [d * d]
        var = sum_dd * (1.0 / HEAD_DIM)
        o = d * lax.rsqrt(var + LNX_EPS) * prow(_P_LNG) + prow(_P_LNB)
        return _bf((o + bonus) * jax.nn.silu(xz)), v_shifted

    def lora_up(c):
        r0 = pl.multiple_of(c * c_, c_)
        cur = sm_ref[pl.ds(r0, c_), :]
        prev_rows = pl.ds(pl.multiple_of(jnp.maximum(r0 - 8, 0), 8), 8)
        last = jnp.where(c > 0, sm_ref[prev_rows, :], carry_sm_scr[...])[7:8]
        prev = jnp.where(first_row_sm, last, pltpu.roll(cur, 1, 0))
        sms = cur + (prev - cur) * mus
        wl = wl_ref[...]
        lw = _dot(_bf(jnp.where(m_wlo, jnp.tanh(sms), 0.0)), wl)
        la = _dot(_bf(jnp.where(m_alo, sms, 0.0)), wl)
        lm = _dot(_bf(jnp.where(m_mv, sms, 0.0)), wl) if has_mix else None
        return lw, la, lm

    pairs = range(wb // pw)
    lane_blk = lambda x, p: None if x is None else x[:, p * pw:(p + 1) * pw]

    def body(step, carry):
        states = [state_scr[p] for p in pairs]
        chunks = [step * RWKV_CHUNKS_PER_STEP + j for j in range(RWKV_CHUNKS_PER_STEP)]
        chains = []
        for j, c in enumerate(chunks):
            lw, la, lm = lora_up(c)
            chains += [pair_chunk(p, load_pair(c, p), lane_blk(lw, p), lane_blk(la, p),
                                  lane_blk(lm, p), states,
                                  max(0, 2 - RWKV_STAGGER) * j) for p in pairs]
        first_round = [(i // len(pairs)) * RWKV_STAGGER for i in range(len(chains))]
        results = [None] * len(chains)
        replies = [None] * len(chains)
        round_idx = -1
        while any(r is None for r in results):
            round_idx += 1
            requests = []
            for i, chain in enumerate(chains):
                if results[i] is not None or round_idx < first_round[i]:
                    continue
                try:
                    wanted = chain.send(replies[i])
                except StopIteration as done:
                    results[i] = done.value
                    continue
                replies[i] = None
                if wanted is not None:
                    requests.append((i, wanted))
            if requests:
                sums = segsum(jnp.concatenate([x for _, xs in requests for x in xs], axis=0))
                row = 0
                for i, xs in requests:
                    replies[i] = [sums[row + k * c_:row + (k + 1) * c_] for k in range(len(xs))]
                    row += len(xs) * c_
        for i, (y, v_shifted) in enumerate(results):
            c, p = chunks[i // len(pairs)], i % len(pairs)
            rows = pl.ds(pl.multiple_of(c * c_, c_), c_)
            lanes = slice(p * pw, (p + 1) * pw)
            y_ref[rows, lanes] = y
            if not has_mix:
                vout_ref[rows, lanes] = v_shifted
        for p in pairs:
            state_scr[p] = states[p]
        return carry

    lax.fori_loop(0, ts // (c_ * RWKV_CHUNKS_PER_STEP), body, 0)

    for i, ref in enumerate(shifted_refs):
        carry_scr[i] = ref[ts - 8:ts, :]
    carry_sm_scr[...] = sm_ref[ts - 8:ts, :]


def _rwkv_call(proj3, small3, v_first, pvec, mu_small, w_lora):
    b, s, _ = proj3.shape
    rw = pvec.shape[1]
    wb, ts = min(RWKV_BLOCK_W, rw), min(RWKV_SEQ_TILE, s)
    groups = rw // wb
    has_mix = v_first is not None
    col0 = 2 * rw // wb

    def col_spec(offset):
        return pl.BlockSpec((None, ts, wb), lambda i, g, j: (i, j, offset + g))

    in_specs = [col_spec(col0), col_spec(col0 + groups), col_spec(col0 + 2 * groups),
                col_spec(col0 + 3 * groups),
                pl.BlockSpec((None, ts, SMALL_W), lambda i, g, j: (i, j, 0))]
    args = [proj3, proj3, proj3, proj3, small3]
    if has_mix:
        in_specs.append(col_spec(0))
        args.append(v_first)
    in_specs += [
        pl.BlockSpec((_P_ROWS, wb), lambda i, g, j: (0, g)),
        pl.BlockSpec((1, SMALL_W), lambda i, g, j: (0, 0)),
        pl.BlockSpec((SMALL_W, wb), lambda i, g, j: (0, g)),
    ]
    args += [pvec, mu_small, w_lora]
    out_specs = [col_spec(0)]
    out_shape = [jax.ShapeDtypeStruct((b, s, rw), BF16)]
    if not has_mix:
        out_specs.append(col_spec(0))
        out_shape.append(jax.ShapeDtypeStruct((b, s, rw), F32))
    res = pl.pallas_call(
        functools.partial(_rwkv_kernel, has_mix),
        grid=(b, groups, s // ts),
        in_specs=in_specs,
        out_specs=out_specs,
        out_shape=out_shape,
        scratch_shapes=[
            pltpu.VMEM((wb // PAIR_W, PAIR_W, PAIR_W), F32),
            pltpu.VMEM((4, 8, wb), F32),
            pltpu.VMEM((8, SMALL_W), F32),
        ],
        compiler_params=_params("parallel", "parallel", "arbitrary"),
        name="rwkv_mixer",
    )(*args)
    return (res[0], v_first) if has_mix else (res[0], res[1])


def _merge_kernel(h_ref, ya_ref, yb_ref, wga_ref, wgb_ref, wa_ref, wb_ref, o_ref,
                  wga_scr, wgb_scr, wa_scr, wb_scr):
    @pl.when(pl.program_id(1) == 0)
    def _():
        wga_scr[...] = _bf(wga_ref[...])
        wgb_scr[...] = _bf(wgb_ref[...])
        wa_scr[...] = _bf(wa_ref[...])
        wb_scr[...] = _bf(wb_ref[...])

    h = h_ref[...]
    ga = jax.nn.sigmoid(_dot(h, wga_scr[...]))
    gb = jax.nn.sigmoid(_dot(h, wgb_scr[...]))
    pa = _dot(ya_ref[...], wa_scr[...])
    pb = _dot(yb_ref[...], wb_scr[...])
    o_ref[...] = _bf(ga * pa + gb * pb)


def _merge_call(h, ya, yb, w_in, gate_col0, w_br_a, w_br_b, layer):
    t, d = h.shape
    ka, kb = ya.shape[1], yb.shape[1]
    tm, tn = min(1024, t), 512
    nj = d // tn
    assert gate_col0 % _LANES == 0
    lane_tiles, ga0, gb0 = tn // _LANES, gate_col0 // _LANES, (gate_col0 + d) // _LANES
    return pl.pallas_call(
        _merge_kernel,
        grid=(nj, t // tm),
        in_specs=[
            pl.BlockSpec((tm, d), lambda j, i: (i, 0)),
            pl.BlockSpec((tm, ka), lambda j, i: (i, 0)),
            pl.BlockSpec((tm, kb), lambda j, i: (i, 0)),
            pl.BlockSpec((pl.Squeezed(), pl.Element(d), pl.Element(tn)),
                         lambda j, i: (layer, 0, (ga0 + j * lane_tiles) * _LANES)),
            pl.BlockSpec((pl.Squeezed(), pl.Element(d), pl.Element(tn)),
                         lambda j, i: (layer, 0, (gb0 + j * lane_tiles) * _LANES)),
            pl.BlockSpec((None, ka, tn), lambda j, i: (layer, 0, j)),
            pl.BlockSpec((None, kb, tn), lambda j, i: (layer, 0, j)),
        ],
        out_specs=pl.BlockSpec((tm, tn), lambda j, i: (i, j)),
        out_shape=jax.ShapeDtypeStruct((t, d), BF16),
        scratch_shapes=[pltpu.VMEM((d, tn), BF16), pltpu.VMEM((d, tn), BF16),
                        pltpu.VMEM((ka, tn), BF16), pltpu.VMEM((kb, tn), BF16)],
        compiler_params=_params("parallel", "arbitrary"),
        name="gated_merge",
    )(h, ya, yb, w_in, w_in, w_br_a, w_br_b)


def _out_kernel(m_ref, w_ref, x_ref, gate_ref, g_ref, o_ref, w_scr):
    @pl.when(pl.program_id(0) == 0)
    def _():
        w_scr[...] = _bf(w_ref[...])

    out = _dot(m_ref[...], w_scr[...])
    y = out * lax.rsqrt(jnp.mean(out * out, axis=-1, keepdims=True) + RMS_EPS)
    o_ref[...] = x_ref[...] + gate_ref[...] * (y * g_ref[...])


def _out_call(merged, w_out, layer, xf, gate, g_post, seq):
    t, d = xf.shape
    tm = 512
    per_seq = seq // tm
    return pl.pallas_call(
        _out_kernel,
        grid=(t // tm,),
        in_specs=[
            pl.BlockSpec((tm, d), lambda i: (i, 0)),
            pl.BlockSpec((None, d, d), lambda i: (layer, 0, 0), pipeline_mode=pl.Buffered(1)),
            pl.BlockSpec((tm, d), lambda i: (i, 0)),
            pl.BlockSpec((None, 1, d), lambda i: (i // per_seq, 0, 0)),
            pl.BlockSpec((1, d), lambda i: (0, 0)),
        ],
        out_specs=pl.BlockSpec((tm, d), lambda i: (i, 0)),
        out_shape=jax.ShapeDtypeStruct((t, d), F32),
        scratch_shapes=[pltpu.VMEM((d, d), BF16)],
        compiler_params=_params("arbitrary"),
        name="out_proj",
    )(merged, w_out, xf, gate, g_post)


def kernel(x, c, w_ada, b_ada, g_pre, w_in, w_pool, pool_scale, mu_shift, w_decay_up, w0, w_aaa_up, a0, w_mv_down, mu_mv, w_mv_up, mv0, k_k, k_a, r_k, lnx_g, lnx_b, w_br_a, w_br_b, w_out, g_post):
    bsz, seq, d = x.shape
    depth = w_ada.shape[0]
    pw = pool_scale.shape[1]
    rw = w0.shape[1]
    t = bsz * seq
    n_main = 2 * pw + 4 * rw
    n_lora = DECAY_LORA + AAA_LORA
    n_shift = 4 * rw + n_lora

    xf = x.reshape(t, d)
    c8 = jnp.pad(c, ((0, 8 - bsz), (0, 0)))
    mod = _mod_call(c8, w_ada, b_ada.reshape(depth, 1, 3 * d))

    v_first = None
    for l in range(depth):
        shift = mod[l, :bsz, :d].reshape(bsz, 1, d)
        scale = mod[l, :bsz, d:2 * d].reshape(bsz, 1, d)
        gate = mod[l, :bsz, 2 * d:].reshape(bsz, 1, d)

        small_cols = [w_in[l, :, n_main:n_main + n_lora]]
        lora_rows = [w_decay_up[l], w_aaa_up[l]]
        mu_small = [mu_shift[l, 4 * rw:n_shift]]
        mv0_l = jnp.zeros((rw,), F32)
        if l > 0:
            small_cols.append(w_mv_down[l - 1])
            lora_rows.append(w_mv_up[l - 1])
            mu_small.append(mu_mv[l - 1])
            mv0_l = mv0[l - 1]
        w_small = jnp.concatenate(small_cols, axis=1)
        w_small = jnp.pad(w_small, ((0, 0), (0, SMALL_W - w_small.shape[1])))
        w_lora = jnp.concatenate(lora_rows, axis=0)
        w_lora = _bf(jnp.pad(w_lora, ((0, SMALL_W - w_lora.shape[0]), (0, 0))))
        mu_small = jnp.concatenate(mu_small)
        mu_small = jnp.pad(mu_small, (0, SMALL_W - mu_small.shape[0])).reshape(1, SMALL_W)
        mu4 = mu_shift[l, :4 * rw].reshape(4, rw)
        pvec = jnp.concatenate(
            [mu4, jnp.stack([w0[l], a0[l], mv0_l, k_k[l], k_a[l], r_k[l].reshape(rw),
                             lnx_g[l], lnx_b[l]]),
             jnp.zeros((_P_ROWS - 12, rw), F32)], axis=0)

        h, small = _prenorm_call(xf, g_pre[l].reshape(1, d), scale, shift, w_small, seq)
        proj = _in_proj_call(h, w_in, l, n_main)
        proj3 = proj.reshape(bsz, seq, n_main)
        small3 = small.reshape(bsz, seq, SMALL_W)

        y_a = _pool_call(proj3, _bf(w_pool[l]), pool_scale[l].reshape(1, pw))
        y_b, v_first = _rwkv_call(proj3, small3, v_first, pvec, mu_small, w_lora)

        merged = _merge_call(h, y_a.reshape(t, pw), y_b.reshape(t, rw), w_in,
                             n_main + n_lora, w_br_a, w_br_b, l)
        xf = _out_call(merged, w_out, l, xf, gate, g_post[l].reshape(1, d), seq)
    return xf.reshape(bsz, seq, d)
```

```python
import functools

import jax
import jax.numpy as jnp
from jax import lax
from jax.experimental import pallas as pl
from jax.experimental.pallas import tpu as pltpu

F32 = jnp.float32
BF16 = jnp.bfloat16

RMS_EPS = 1e-6
LNX_EPS = 64e-5
POOL_WINDOWS = (2, 4, 8, 16)
HEAD_DIM = 64
DECAY_LORA = 64
AAA_LORA = 64
MV_LORA = 32

CHUNK = 64
PAIR_HEADS = 2
PAIR_W = PAIR_HEADS * HEAD_DIM
RWKV_BLOCK_W = 1024
RWKV_SEQ_TILE = 512
RWKV_CHUNKS_PER_STEP = 8
RWKV_STAGGER = 5
SMALL_W = 256

_LANES = 128
_MIB = 1024 * 1024
_V7X_VMEM_LIMIT = 56 * _MIB

_EXP_M_HALF = 0.6065306597126334
_NT = (((1,), (1,)), ((), ()))
_TN = (((0,), (0,)), ((), ()))


def _params(*sem):
    return pltpu.CompilerParams(dimension_semantics=sem, vmem_limit_bytes=_V7X_VMEM_LIMIT)


def _dot(a, b):
    return jnp.dot(a, b, preferred_element_type=F32)


def _bf(x):
    return x.astype(BF16)


def _mod_kernel(c_ref, w_ref, b_ref, o_ref):
    cond = _bf(jax.nn.silu(c_ref[...]))
    o_ref[...] = _dot(cond, _bf(w_ref[...])) + b_ref[...]


def _mod_call(c8, w_ada, b_ada3):
    depth, d, n = w_ada.shape
    tn = 1024
    return pl.pallas_call(
        _mod_kernel,
        grid=(depth, n // tn),
        in_specs=[
            pl.BlockSpec((8, d), lambda l, j: (0, 0)),
            pl.BlockSpec((None, d, tn), lambda l, j: (l, 0, j)),
            pl.BlockSpec((None, 1, tn), lambda l, j: (l, 0, j)),
        ],
        out_specs=pl.BlockSpec((None, 8, tn), lambda l, j: (l, 0, j)),
        out_shape=jax.ShapeDtypeStruct((depth, 8, n), F32),
        compiler_params=_params("parallel", "parallel"),
        name="adaln_mod",
    )(c8, w_ada, b_ada3)


def _prenorm_kernel(x_ref, g_ref, sc_ref, sh_ref, ws_ref, h_ref, sm_ref):
    x = x_ref[...]
    y = x * lax.rsqrt(jnp.mean(x * x, axis=-1, keepdims=True) + RMS_EPS)
    h = _bf((y * g_ref[...]) * (1.0 + sc_ref[...]) + sh_ref[...])
    h_ref[...] = h
    sm_ref[...] = _dot(h, _bf(ws_ref[...]))


def _prenorm_call(xf, g, scale, shift, w_small, seq):
    t, d = xf.shape
    tm = 1024
    per_seq = seq // tm
    return pl.pallas_call(
        _prenorm_kernel,
        grid=(t // tm,),
        in_specs=[
            pl.BlockSpec((tm, d), lambda i: (i, 0)),
            pl.BlockSpec((1, d), lambda i: (0, 0)),
            pl.BlockSpec((None, 1, d), lambda i: (i // per_seq, 0, 0)),
            pl.BlockSpec((None, 1, d), lambda i: (i // per_seq, 0, 0)),
            pl.BlockSpec((d, SMALL_W), lambda i: (0, 0)),
        ],
        out_specs=[
            pl.BlockSpec((tm, d), lambda i: (i, 0)),
            pl.BlockSpec((tm, SMALL_W), lambda i: (i, 0)),
        ],
        out_shape=[
            jax.ShapeDtypeStruct((t, d), BF16),
            jax.ShapeDtypeStruct((t, SMALL_W), F32),
        ],
        compiler_params=_params("parallel"),
        name="prenorm",
    )(xf, g, scale, shift, w_small)


def _in_proj_kernel(h_ref, w_ref, o_ref, w_scr):
    @pl.when(pl.program_id(1) == 0)
    def _():
        w_scr[...] = _bf(w_ref[...])

    o_ref[...] = _dot(h_ref[...], w_scr[...])


def _in_proj_call(h, w_in, layer, n_cols):
    m, k = h.shape
    tm, tn = min(1024, m), 1024
    return pl.pallas_call(
        _in_proj_kernel,
        grid=(n_cols // tn, m // tm),
        in_specs=[
            pl.BlockSpec((tm, k), lambda j, i: (i, 0)),
            pl.BlockSpec((None, k, tn), lambda j, i: (layer, 0, j)),
        ],
        out_specs=pl.BlockSpec((tm, tn), lambda j, i: (i, j)),
        out_shape=jax.ShapeDtypeStruct((m, n_cols), F32),
        scratch_shapes=[pltpu.VMEM((k, tn), BF16)],
        compiler_params=_params("parallel", "arbitrary"),
        name="in_proj",
    )(h, w_in)


def _pool_kernel(u_ref, z_ref, w_ref, ps_ref, y_ref):
    group = pl.program_id(1)
    u = u_ref[...]
    row = lax.broadcasted_iota(jnp.int32, (u.shape[0], 1), 0)

    def shifted(a, s):
        return jnp.where(row >= s, pltpu.roll(a, s, 0), 0.0)

    for gi, window in enumerate(POOL_WINDOWS):

        @pl.when(group == gi)
        def _():
            acc, span = u, 1
            while span < window:
                acc = acc + shifted(acc, span)
                span *= 2
            cnt = jnp.minimum(row + 1, window).astype(F32)
            pooled = acc / cnt - u
            y = _dot(_bf(pooled), w_ref[...])
            y_ref[...] = _bf(y * ps_ref[...] * jax.nn.silu(z_ref[...]))


def _pool_call(proj3, w_pool, pool_scale):
    b, s, _ = proj3.shape
    groups, gd, _ = w_pool.shape
    return pl.pallas_call(
        _pool_kernel,
        grid=(b, groups),
        in_specs=[
            pl.BlockSpec((None, s, gd), lambda i, g: (i, 0, g)),
            pl.BlockSpec((None, s, gd), lambda i, g: (i, 0, groups + g)),
            pl.BlockSpec((None, gd, gd), lambda i, g: (g, 0, 0)),
            pl.BlockSpec((1, gd), lambda i, g: (0, g)),
        ],
        out_specs=pl.BlockSpec((None, s, gd), lambda i, g: (i, 0, g)),
        out_shape=jax.ShapeDtypeStruct((b, s, groups * gd), BF16),
        compiler_params=_params("parallel", "parallel"),
        name="pool_mixer",
    )(proj3, proj3, w_pool, pool_scale)


(_P_MU_R, _P_MU_K, _P_MU_V, _P_MU_Z, _P_W0, _P_A0, _P_MV0, _P_KK, _P_KA, _P_RK,
 _P_LNG, _P_LNB) = range(12)
_P_ROWS = 16


def _split_terms(x, terms):
    pieces, rem = [], x
    for _ in range(terms):
        p = _bf(rem)
        pieces.append(p)
        rem = rem - p.astype(F32)
    return pieces


def _rwkv_kernel(has_mix, *refs):
    if has_mix:
        (r_ref, k_ref, v_ref, z_ref, sm_ref, vf_ref, pv_ref, mus_ref, wl_ref,
         y_ref, state_scr, carry_scr, carry_sm_scr) = refs
    else:
        (r_ref, k_ref, v_ref, z_ref, sm_ref, pv_ref, mus_ref, wl_ref,
         y_ref, vout_ref, state_scr, carry_scr, carry_sm_scr) = refs
    ts, wb = r_ref.shape
    c_, pw, st = CHUNK, PAIR_W, PAIR_HEADS * CHUNK
    shifted_refs = (r_ref, k_ref, v_ref, z_ref)

    @pl.when(pl.program_id(2) == 0)
    def _():
        state_scr[...] = jnp.zeros_like(state_scr)
        carry_scr[...] = jnp.zeros_like(carry_scr)
        carry_sm_scr[...] = jnp.zeros_like(carry_sm_scr)

    pv = pv_ref[...]
    mus = mus_ref[...]

    def iota(shape, dim):
        return lax.broadcasted_iota(jnp.int32, shape, dim)

    first_row = iota((c_, pw), 0) == 0
    lane = iota((c_, pw), 1)
    head_masks = [lane // HEAD_DIM == j for j in range(PAIR_HEADS)]
    sq_r, sq_c = iota((st, pw), 0), iota((st, pw), 1)
    blockdiag = sq_r // HEAD_DIM == sq_c // HEAD_DIM
    ones_bd = jnp.where(blockdiag, 1.0, 0.0).astype(BF16)
    ones_bd2 = jnp.concatenate([ones_bd, ones_bd], axis=0)
    tok_r, tok_c = iota((c_, 2 * st), 0), iota((c_, 2 * st), 1) % c_
    strict_lower = (tok_r > tok_c)[:, :st]
    lower = tok_r >= tok_c
    eye = jnp.where(tok_r == tok_c, 1.0, 0.0).astype(F32)[:, :st]
    tri = jnp.where(iota((c_, c_), 0) >= iota((c_, c_), 1), 1.0, 0.0).astype(BF16)
    tri2 = jnp.concatenate([tri, tri], axis=1)

    def segsum(x):
        return _dot(jnp.concatenate(_split_terms(x, 2), axis=1), ones_bd2)

    def stack(x):
        return _bf(jnp.concatenate([jnp.where(m, x, 0.0) for m in head_masks], axis=0))

    def blockdiag_of(x):
        return _bf(jnp.where(blockdiag, jnp.concatenate([x] * PAIR_HEADS, axis=0), 0.0))

    def load_pair(c, p):
        lanes = slice(p * pw, (p + 1) * pw)
        r0 = pl.multiple_of(c * c_, c_)
        rows = pl.ds(r0, c_)
        prev_rows = pl.ds(pl.multiple_of(jnp.maximum(r0 - 8, 0), 8), 8)
        cur = [ref[rows, lanes] for ref in shifted_refs]
        last = [jnp.where(c > 0, ref[prev_rows, lanes], carry_scr[i, :, lanes])[7:8]
                for i, ref in enumerate(shifted_refs)]
        vf = vf_ref[rows, lanes] if has_mix else None
        return cur, last, vf

    def pair_chunk(p, loaded, lw, la, lm, states, state_delay):
        lanes = slice(p * pw, (p + 1) * pw)
        cur, last, vf = loaded
        prow = lambda i: pv[i:i + 1, lanes]

        def tshift(i, mu):
            prev = jnp.where(first_row, last[i], pltpu.roll(cur[i], 1, 0))
            return cur[i] + (prev - cur[i]) * mu

        xr = tshift(0, prow(_P_MU_R))
        xk = tshift(1, prow(_P_MU_K))
        xv = tshift(2, prow(_P_MU_V))
        xz = tshift(3, prow(_P_MU_Z))
        v_shifted = xv

        ld = -_EXP_M_HALF * jax.nn.sigmoid(prow(_P_W0) + lw)
        asig = jax.nn.sigmoid(prow(_P_A0) + la)
        if has_mix:
            xv = xv + (vf - xv) * jax.nn.sigmoid(prow(_P_MV0) + lm)
        kk = xk * prow(_P_KK)
        kp = xk * (1.0 + (asig - 1.0) * prow(_P_KA))
        cum = _dot(tri2, jnp.concatenate(_split_terms(ld, 2), axis=0))
        kk_sq, rk_sum = yield [kk * kk, xr * kp * prow(_P_RK)]
        kkn = kk * lax.rsqrt(jnp.maximum(kk_sq, 1e-24))
        a = -kkn
        b = kkn * asig
        bonus = rk_sum * xv
        e_in = jnp.exp(cum)
        e_out = jnp.exp(-cum)
        w_end = e_in[c_ - 1:c_, :]
        e_end = w_end * e_out
        a_t = a * jnp.exp(cum - ld)
        r_t = xr * e_in
        b_t = b * e_out
        k_t = kp * e_out
        b_c = _bf(b * e_end)
        k_c = _bf(kp * e_end)
        v_st = stack(xv)

        g = lax.dot_general(_bf(jnp.concatenate([a_t, r_t], axis=0)),
                            jnp.concatenate([stack(b_t), stack(k_t)], axis=0), _NT,
                            preferred_element_type=F32)
        a_ab = jnp.where(strict_lower, g[:c_, :st], 0.0)
        a_ak = jnp.where(strict_lower, g[:c_, st:], 0.0)
        a_r = _bf(jnp.where(lower, g[c_:, :], 0.0))
        yield

        akv = _dot(_bf(a_ak), v_st)
        t_inv = eye + a_ab
        a_pow = _dot(_bf(a_ab), blockdiag_of(a_ab))
        yield
        span = 2
        while span < c_:
            pow_bd = blockdiag_of(a_pow)
            span *= 2
            if span < c_:
                both = _dot(_bf(jnp.concatenate([t_inv, a_pow], axis=0)), pow_bd)
                t_inv, a_pow = t_inv + both[:c_], both[c_:]
            else:
                t_inv = t_inv + _dot(_bf(t_inv), pow_bd)
            yield

        x1 = _dot(_bf(t_inv), jnp.concatenate([stack(a_t), stack(akv)], axis=1))
        yield
        a_p, u0 = x1[:, :pw], x1[:, pw:]
        x2 = _dot(a_r[:, :st], jnp.concatenate([stack(a_p), stack(u0)], axis=1))
        x3 = _dot(a_r[:, st:], v_st)
        yield
        r_p = r_t + x2[:, :pw]
        y0 = x2[:, pw:] + x3
        for _ in range(state_delay):
            yield

        state = states[p]
        ou = lax.dot_general(_bf(jnp.concatenate([r_p, a_p], axis=0)), _bf(state), _NT,
                             preferred_element_type=F32)
        yield
        o = ou[:c_] + y0
        u = ou[c_:] + u0
        states[p] = state * w_end + jnp.where(
            blockdiag,
            lax.dot_general(_bf(jnp.concatenate([u, xv], axis=0)),
                            jnp.concatenate([b_c, k_c], axis=0), _TN,
                            preferred_element_type=F32), 0.0)

        (sum_o,) = yield [o]
        d = o - sum_o * (1.0 / HEAD_DIM)
        (sum_dd,) = yield [d * d]
        var = sum_dd * (1.0 / HEAD_DIM)
        o = d * lax.rsqrt(var + LNX_EPS) * prow(_P_LNG) + prow(_P_LNB)
        return _bf((o + bonus) * jax.nn.silu(xz)), v_shifted

    step_rows = c_ * RWKV_CHUNKS_PER_STEP

    def lora_up(step):
        r0 = pl.multiple_of(step * step_rows, step_rows)
        cur = sm_ref[pl.ds(r0, step_rows), :]
        prev_rows = pl.ds(pl.multiple_of(jnp.maximum(r0 - 8, 0), 8), 8)
        last = jnp.where(step > 0, sm_ref[prev_rows, :], carry_sm_scr[...])[7:8]
        first = lax.broadcasted_iota(jnp.int32, cur.shape, 0) == 0
        lane_all = lax.broadcasted_iota(jnp.int32, cur.shape, 1)
        prev = jnp.where(first, last, pltpu.roll(cur, 1, 0))
        sms = cur + (prev - cur) * mus
        wl = wl_ref[...]
        in_cols = lambda lo, n: (lane_all >= lo) & (lane_all < lo + n)
        lw = _dot(_bf(jnp.where(in_cols(0, DECAY_LORA), jnp.tanh(sms), 0.0)), wl)
        la = _dot(_bf(jnp.where(in_cols(DECAY_LORA, AAA_LORA), sms, 0.0)), wl)
        lm = (_dot(_bf(jnp.where(in_cols(DECAY_LORA + AAA_LORA, MV_LORA), sms, 0.0)), wl)
              if has_mix else None)
        return lw, la, lm

    pairs = range(wb // pw)

    def chunk_pair_blk(x, j, p):
        return None if x is None else x[j * c_:(j + 1) * c_, p * pw:(p + 1) * pw]

    def body(step, carry):
        states = [state_scr[p] for p in pairs]
        chunks = [step * RWKV_CHUNKS_PER_STEP + j for j in range(RWKV_CHUNKS_PER_STEP)]
        chains = []
        lw, la, lm = lora_up(step)
        for j, c in enumerate(chunks):
            chains += [pair_chunk(p, load_pair(c, p), chunk_pair_blk(lw, j, p),
                                  chunk_pair_blk(la, j, p), chunk_pair_blk(lm, j, p), states,
                                  max(0, 2 - RWKV_STAGGER) * j) for p in pairs]
        first_round = [(i // len(pairs)) * RWKV_STAGGER for i in range(len(chains))]
        results = [None] * len(chains)
        replies = [None] * len(chains)
        round_idx = -1
        while any(r is None for r in results):
            round_idx += 1
            requests = []
            for i, chain in enumerate(chains):
                if results[i] is not None or round_idx < first_round[i]:
                    continue
                try:
                    wanted = chain.send(replies[i])
                except StopIteration as done:
                    results[i] = done.value
                    continue
                replies[i] = None
                if wanted is not None:
                    requests.append((i, wanted))
            if requests:
                sums = segsum(jnp.concatenate([x for _, xs in requests for x in xs], axis=0))
                row = 0
                for i, xs in requests:
                    replies[i] = [sums[row + k * c_:row + (k + 1) * c_] for k in range(len(xs))]
                    row += len(xs) * c_
        for i, (y, v_shifted) in enumerate(results):
            c, p = chunks[i // len(pairs)], i % len(pairs)
            rows = pl.ds(pl.multiple_of(c * c_, c_), c_)
            lanes = slice(p * pw, (p + 1) * pw)
            y_ref[rows, lanes] = y
            if not has_mix:
                vout_ref[rows, lanes] = v_shifted
        for p in pairs:
            state_scr[p] = states[p]
        return carry

    lax.fori_loop(0, ts // (c_ * RWKV_CHUNKS_PER_STEP), body, 0)

    for i, ref in enumerate(shifted_refs):
        carry_scr[i] = ref[ts - 8:ts, :]
    carry_sm_scr[...] = sm_ref[ts - 8:ts, :]


def _rwkv_call(proj3, small3, v_first, pvec, mu_small, w_lora):
    b, s, _ = proj3.shape
    rw = pvec.shape[1]
    wb, ts = min(RWKV_BLOCK_W, rw), min(RWKV_SEQ_TILE, s)
    groups = rw // wb
    has_mix = v_first is not None
    col0 = 2 * rw // wb

    def col_spec(offset):
        return pl.BlockSpec((None, ts, wb), lambda i, g, j: (i, j, offset + g))

    in_specs = [col_spec(col0), col_spec(col0 + groups), col_spec(col0 + 2 * groups),
                col_spec(col0 + 3 * groups),
                pl.BlockSpec((None, ts, SMALL_W), lambda i, g, j: (i, j, 0))]
    args = [proj3, proj3, proj3, proj3, small3]
    if has_mix:
        in_specs.append(col_spec(0))
        args.append(v_first)
    in_specs += [
        pl.BlockSpec((_P_ROWS, wb), lambda i, g, j: (0, g)),
        pl.BlockSpec((1, SMALL_W), lambda i, g, j: (0, 0)),
        pl.BlockSpec((SMALL_W, wb), lambda i, g, j: (0, g)),
    ]
    args += [pvec, mu_small, w_lora]
    out_specs = [col_spec(0)]
    out_shape = [jax.ShapeDtypeStruct((b, s, rw), BF16)]
    if not has_mix:
        out_specs.append(col_spec(0))
        out_shape.append(jax.ShapeDtypeStruct((b, s, rw), F32))
    res = pl.pallas_call(
        functools.partial(_rwkv_kernel, has_mix),
        grid=(b, groups, s // ts),
        in_specs=in_specs,
        out_specs=out_specs,
        out_shape=out_shape,
        scratch_shapes=[
            pltpu.VMEM((wb // PAIR_W, PAIR_W, PAIR_W), F32),
            pltpu.VMEM((4, 8, wb), F32),
            pltpu.VMEM((8, SMALL_W), F32),
        ],
        compiler_params=_params("parallel", "parallel", "arbitrary"),
        name="rwkv_mixer",
    )(*args)
    return (res[0], v_first) if has_mix else (res[0], res[1])


def _merge_kernel(h_ref, ya_ref, yb_ref, wga_ref, wgb_ref, wa_ref, wb_ref, o_ref,
                  wga_scr, wgb_scr, wa_scr, wb_scr):
    @pl.when(pl.program_id(1) == 0)
    def _():
        wga_scr[...] = _bf(wga_ref[...])
        wgb_scr[...] = _bf(wgb_ref[...])
        wa_scr[...] = _bf(wa_ref[...])
        wb_scr[...] = _bf(wb_ref[...])

    h = h_ref[...]
    ga = jax.nn.sigmoid(_dot(h, wga_scr[...]))
    gb = jax.nn.sigmoid(_dot(h, wgb_scr[...]))
    pa = _dot(ya_ref[...], wa_scr[...])
    pb = _dot(yb_ref[...], wb_scr[...])
    o_ref[...] = _bf(ga * pa + gb * pb)


def _merge_call(h, ya, yb, w_in, gate_col0, w_br_a, w_br_b, layer):
    t, d = h.shape
    ka, kb = ya.shape[1], yb.shape[1]
    tm, tn = min(1024, t), 512
    nj = d // tn
    assert gate_col0 % _LANES == 0
    lane_tiles, ga0, gb0 = tn // _LANES, gate_col0 // _LANES, (gate_col0 + d) // _LANES
    return pl.pallas_call(
        _merge_kernel,
        grid=(nj, t // tm),
        in_specs=[
            pl.BlockSpec((tm, d), lambda j, i: (i, 0)),
            pl.BlockSpec((tm, ka), lambda j, i: (i, 0)),
            pl.BlockSpec((tm, kb), lambda j, i: (i, 0)),
            pl.BlockSpec((pl.Squeezed(), pl.Element(d), pl.Element(tn)),
                         lambda j, i: (layer, 0, (ga0 + j * lane_tiles) * _LANES)),
            pl.BlockSpec((pl.Squeezed(), pl.Element(d), pl.Element(tn)),
                         lambda j, i: (layer, 0, (gb0 + j * lane_tiles) * _LANES)),
            pl.BlockSpec((None, ka, tn), lambda j, i: (layer, 0, j)),
            pl.BlockSpec((None, kb, tn), lambda j, i: (layer, 0, j)),
        ],
        out_specs=pl.BlockSpec((tm, tn), lambda j, i: (i, j)),
        out_shape=jax.ShapeDtypeStruct((t, d), BF16),
        scratch_shapes=[pltpu.VMEM((d, tn), BF16), pltpu.VMEM((d, tn), BF16),
                        pltpu.VMEM((ka, tn), BF16), pltpu.VMEM((kb, tn), BF16)],
        compiler_params=_params("parallel", "arbitrary"),
        name="gated_merge",
    )(h, ya, yb, w_in, w_in, w_br_a, w_br_b)


def _out_kernel(m_ref, w_ref, x_ref, gate_ref, g_ref, o_ref, w_scr):
    @pl.when(pl.program_id(0) == 0)
    def _():
        w_scr[...] = _bf(w_ref[...])

    out = _dot(m_ref[...], w_scr[...])
    y = out * lax.rsqrt(jnp.mean(out * out, axis=-1, keepdims=True) + RMS_EPS)
    o_ref[...] = x_ref[...] + gate_ref[...] * (y * g_ref[...])


def _out_call(merged, w_out, layer, xf, gate, g_post, seq):
    t, d = xf.shape
    tm = 512
    per_seq = seq // tm
    return pl.pallas_call(
        _out_kernel,
        grid=(t // tm,),
        in_specs=[
            pl.BlockSpec((tm, d), lambda i: (i, 0)),
            pl.BlockSpec((None, d, d), lambda i: (layer, 0, 0), pipeline_mode=pl.Buffered(1)),
            pl.BlockSpec((tm, d), lambda i: (i, 0)),
            pl.BlockSpec((None, 1, d), lambda i: (i // per_seq, 0, 0)),
            pl.BlockSpec((1, d), lambda i: (0, 0)),
        ],
        out_specs=pl.BlockSpec((tm, d), lambda i: (i, 0)),
        out_shape=jax.ShapeDtypeStruct((t, d), F32),
        scratch_shapes=[pltpu.VMEM((d, d), BF16)],
        compiler_params=_params("arbitrary"),
        name="out_proj",
    )(merged, w_out, xf, gate, g_post)


def kernel(x, c, w_ada, b_ada, g_pre, w_in, w_pool, pool_scale, mu_shift, w_decay_up, w0, w_aaa_up, a0, w_mv_down, mu_mv, w_mv_up, mv0, k_k, k_a, r_k, lnx_g, lnx_b, w_br_a, w_br_b, w_out, g_post):
    bsz, seq, d = x.shape
    depth = w_ada.shape[0]
    pw = pool_scale.shape[1]
    rw = w0.shape[1]
    t = bsz * seq
    n_main = 2 * pw + 4 * rw
    n_lora = DECAY_LORA + AAA_LORA
    n_shift = 4 * rw + n_lora

    xf = x.reshape(t, d)
    c8 = jnp.pad(c, ((0, 8 - bsz), (0, 0)))
    mod = _mod_call(c8, w_ada, b_ada.reshape(depth, 1, 3 * d))

    v_first = None
    for l in range(depth):
        shift = mod[l, :bsz, :d].reshape(bsz, 1, d)
        scale = mod[l, :bsz, d:2 * d].reshape(bsz, 1, d)
        gate = mod[l, :bsz, 2 * d:].reshape(bsz, 1, d)

        small_cols = [w_in[l, :, n_main:n_main + n_lora]]
        lora_rows = [w_decay_up[l], w_aaa_up[l]]
        mu_small = [mu_shift[l, 4 * rw:n_shift]]
        mv0_l = jnp.zeros((rw,), F32)
        if l > 0:
            small_cols.append(w_mv_down[l - 1])
            lora_rows.append(w_mv_up[l - 1])
            mu_small.append(mu_mv[l - 1])
            mv0_l = mv0[l - 1]
        w_small = jnp.concatenate(small_cols, axis=1)
        w_small = jnp.pad(w_small, ((0, 0), (0, SMALL_W - w_small.shape[1])))
        w_lora = jnp.concatenate(lora_rows, axis=0)
        w_lora = _bf(jnp.pad(w_lora, ((0, SMALL_W - w_lora.shape[0]), (0, 0))))
        mu_small = jnp.concatenate(mu_small)
        mu_small = jnp.pad(mu_small, (0, SMALL_W - mu_small.shape[0])).reshape(1, SMALL_W)
        mu4 = mu_shift[l, :4 * rw].reshape(4, rw)
        pvec = jnp.concatenate(
            [mu4, jnp.stack([w0[l], a0[l], mv0_l, k_k[l], k_a[l], r_k[l].reshape(rw),
                             lnx_g[l], lnx_b[l]]),
             jnp.zeros((_P_ROWS - 12, rw), F32)], axis=0)

        h, small = _prenorm_call(xf, g_pre[l].reshape(1, d), scale, shift, w_small, seq)
        proj = _in_proj_call(h, w_in, l, n_main)
        proj3 = proj.reshape(bsz, seq, n_main)
        small3 = small.reshape(bsz, seq, SMALL_W)

        y_a = _pool_call(proj3, _bf(w_pool[l]), pool_scale[l].reshape(1, pw))
        y_b, v_first = _rwkv_call(proj3, small3, v_first, pvec, mu_small, w_lora)

        merged = _merge_call(h, y_a.reshape(t, pw), y_b.reshape(t, rw), w_in,
                             n_main + n_lora, w_br_a, w_br_b, l)
        xf = _out_call(merged, w_out, l, xf, gate, g_post[l].reshape(1, d), seq)
    return xf.reshape(bsz, seq, d)
```

```python
import functools

import jax
import jax.numpy as jnp
from jax import lax
from jax.experimental import pallas as pl
from jax.experimental.pallas import tpu as pltpu

F32 = jnp.float32
BF16 = jnp.bfloat16

RMS_EPS = 1e-6
LNX_EPS = 64e-5
POOL_WINDOWS = (2, 4, 8, 16)
HEAD_DIM = 64
DECAY_LORA = 64
AAA_LORA = 64
MV_LORA = 32

CHUNK = 64
PAIR_HEADS = 2
PAIR_W = PAIR_HEADS * HEAD_DIM
RWKV_BLOCK_W = 1024
RWKV_SEQ_TILE = 512
RWKV_CHUNKS_PER_STEP = 8
RWKV_STAGGER = 5
SMALL_W = 256

_LANES = 128
_MIB = 1024 * 1024
_V7X_VMEM_LIMIT = 56 * _MIB

_EXP_M_HALF = 0.6065306597126334
_NT = (((1,), (1,)), ((), ()))
_TN = (((0,), (0,)), ((), ()))


def _params(*sem):
    return pltpu.CompilerParams(dimension_semantics=sem, vmem_limit_bytes=_V7X_VMEM_LIMIT)


def _dot(a, b):
    return jnp.dot(a, b, preferred_element_type=F32)


def _bf(x):
    return x.astype(BF16)


def _mod_kernel(c_ref, w_ref, b_ref, o_ref):
    cond = _bf(jax.nn.silu(c_ref[...]))
    o_ref[...] = _dot(cond, _bf(w_ref[...])) + b_ref[...]


def _mod_call(c8, w_ada, b_ada3):
    depth, d, n = w_ada.shape
    tn = 1024
    return pl.pallas_call(
        _mod_kernel,
        grid=(depth, n // tn),
        in_specs=[
            pl.BlockSpec((8, d), lambda l, j: (0, 0)),
            pl.BlockSpec((None, d, tn), lambda l, j: (l, 0, j)),
            pl.BlockSpec((None, 1, tn), lambda l, j: (l, 0, j)),
        ],
        out_specs=pl.BlockSpec((None, 8, tn), lambda l, j: (l, 0, j)),
        out_shape=jax.ShapeDtypeStruct((depth, 8, n), F32),
        compiler_params=_params("parallel", "parallel"),
        name="adaln_mod",
    )(c8, w_ada, b_ada3)


def _prenorm_kernel(x_ref, g_ref, sc_ref, sh_ref, ws_ref, h_ref, sm_ref):
    x = x_ref[...]
    y = x * lax.rsqrt(jnp.mean(x * x, axis=-1, keepdims=True) + RMS_EPS)
    h = _bf((y * g_ref[...]) * (1.0 + sc_ref[...]) + sh_ref[...])
    h_ref[...] = h
    sm_ref[...] = _dot(h, _bf(ws_ref[...]))


def _prenorm_call(xf, g, scale, shift, w_small, seq):
    t, d = xf.shape
    tm = 1024
    per_seq = seq // tm
    return pl.pallas_call(
        _prenorm_kernel,
        grid=(t // tm,),
        in_specs=[
            pl.BlockSpec((tm, d), lambda i: (i, 0)),
            pl.BlockSpec((1, d), lambda i: (0, 0)),
            pl.BlockSpec((None, 1, d), lambda i: (i // per_seq, 0, 0)),
            pl.BlockSpec((None, 1, d), lambda i: (i // per_seq, 0, 0)),
            pl.BlockSpec((d, SMALL_W), lambda i: (0, 0)),
        ],
        out_specs=[
            pl.BlockSpec((tm, d), lambda i: (i, 0)),
            pl.BlockSpec((tm, SMALL_W), lambda i: (i, 0)),
        ],
        out_shape=[
            jax.ShapeDtypeStruct((t, d), BF16),
            jax.ShapeDtypeStruct((t, SMALL_W), F32),
        ],
        compiler_params=_params("parallel"),
        name="prenorm",
    )(xf, g, scale, shift, w_small)


def _in_proj_kernel(h_ref, w_ref, o_ref, w_scr):
    @pl.when(pl.program_id(1) == 0)
    def _():
        w_scr[...] = _bf(w_ref[...])

    o_ref[...] = _dot(h_ref[...], w_scr[...])


def _in_proj_call(h, w_in, layer, n_cols):
    m, k = h.shape
    tm, tn = min(1024, m), 1024
    return pl.pallas_call(
        _in_proj_kernel,
        grid=(n_cols // tn, m // tm),
        in_specs=[
            pl.BlockSpec((tm, k), lambda j, i: (i, 0)),
            pl.BlockSpec((None, k, tn), lambda j, i: (layer, 0, j)),
        ],
        out_specs=pl.BlockSpec((tm, tn), lambda j, i: (i, j)),
        out_shape=jax.ShapeDtypeStruct((m, n_cols), F32),
        scratch_shapes=[pltpu.VMEM((k, tn), BF16)],
        compiler_params=_params("parallel", "arbitrary"),
        name="in_proj",
    )(h, w_in)


def _pool_kernel(u_ref, z_ref, w_ref, ps_ref, y_ref):
    group = pl.program_id(1)
    u = u_ref[...]
    row = lax.broadcasted_iota(jnp.int32, (u.shape[0], 1), 0)

    def shifted(a, s):
        return jnp.where(row >= s, pltpu.roll(a, s, 0), 0.0)

    for gi, window in enumerate(POOL_WINDOWS):

        @pl.when(group == gi)
        def _():
            acc, span = u, 1
            while span < window:
                acc = acc + shifted(acc, span)
                span *= 2
            cnt = jnp.minimum(row + 1, window).astype(F32)
            pooled = acc / cnt - u
            y = _dot(_bf(pooled), w_ref[...])
            y_ref[...] = _bf(y * ps_ref[...] * jax.nn.silu(z_ref[...]))


def _pool_call(proj3, w_pool, pool_scale):
    b, s, _ = proj3.shape
    groups, gd, _ = w_pool.shape
    return pl.pallas_call(
        _pool_kernel,
        grid=(b, groups),
        in_specs=[
            pl.BlockSpec((None, s, gd), lambda i, g: (i, 0, g)),
            pl.BlockSpec((None, s, gd), lambda i, g: (i, 0, groups + g)),
            pl.BlockSpec((None, gd, gd), lambda i, g: (g, 0, 0)),
            pl.BlockSpec((1, gd), lambda i, g: (0, g)),
        ],
        out_specs=pl.BlockSpec((None, s, gd), lambda i, g: (i, 0, g)),
        out_shape=jax.ShapeDtypeStruct((b, s, groups * gd), BF16),
        compiler_params=_params("parallel", "parallel"),
        name="pool_mixer",
    )(proj3, proj3, w_pool, pool_scale)


(_P_MU_R, _P_MU_K, _P_MU_V, _P_MU_Z, _P_W0, _P_A0, _P_MV0, _P_KK, _P_KA, _P_RK,
 _P_LNG, _P_LNB) = range(12)
_P_ROWS = 16


def _split_terms(x, terms):
    pieces, rem = [], x
    for _ in range(terms):
        p = _bf(rem)
        pieces.append(p)
        rem = rem - p.astype(F32)
    return pieces


def _rwkv_kernel(has_mix, *refs):
    if has_mix:
        (r_ref, k_ref, v_ref, z_ref, sm_ref, vf_ref, pv_ref, mus_ref, wl_ref,
         y_ref, state_scr, carry_scr, carry_sm_scr) = refs
    else:
        (r_ref, k_ref, v_ref, z_ref, sm_ref, pv_ref, mus_ref, wl_ref,
         y_ref, vout_ref, state_scr, carry_scr, carry_sm_scr) = refs
    ts, wb = r_ref.shape
    c_, pw, st = CHUNK, PAIR_W, PAIR_HEADS * CHUNK
    shifted_refs = (r_ref, k_ref, v_ref, z_ref)

    @pl.when(pl.program_id(2) == 0)
    def _():
        state_scr[...] = jnp.zeros_like(state_scr)
        carry_scr[...] = jnp.zeros_like(carry_scr)
        carry_sm_scr[...] = jnp.zeros_like(carry_sm_scr)

    pv = pv_ref[...]
    mus = mus_ref[...]

    def iota(shape, dim):
        return lax.broadcasted_iota(jnp.int32, shape, dim)

    first_row = iota((c_, pw), 0) == 0
    lane = iota((c_, pw), 1)
    head_masks = [lane // HEAD_DIM == j for j in range(PAIR_HEADS)]
    sq_r, sq_c = iota((st, pw), 0), iota((st, pw), 1)
    blockdiag = sq_r // HEAD_DIM == sq_c // HEAD_DIM
    ones_bd = jnp.where(blockdiag, 1.0, 0.0).astype(BF16)
    ones_bd2 = jnp.concatenate([ones_bd, ones_bd], axis=0)
    tok_r, tok_c = iota((c_, 2 * st), 0), iota((c_, 2 * st), 1) % c_
    strict_lower = (tok_r > tok_c)[:, :st]
    lower = tok_r >= tok_c
    eye = jnp.where(tok_r == tok_c, 1.0, 0.0).astype(F32)[:, :st]
    tri = jnp.where(iota((c_, c_), 0) >= iota((c_, c_), 1), 1.0, 0.0).astype(BF16)
    tri2 = jnp.concatenate([tri, tri], axis=1)

    def segsum(x):
        return _dot(jnp.concatenate(_split_terms(x, 2), axis=1), ones_bd2)

    def stack(x):
        return _bf(jnp.concatenate([jnp.where(m, x, 0.0) for m in head_masks], axis=0))

    def blockdiag_of(x):
        return _bf(jnp.where(blockdiag, jnp.concatenate([x] * PAIR_HEADS, axis=0), 0.0))

    def load_pair(c, p):
        lanes = slice(p * pw, (p + 1) * pw)
        r0 = pl.multiple_of(c * c_, c_)
        rows = pl.ds(r0, c_)
        prev_rows = pl.ds(pl.multiple_of(jnp.maximum(r0 - 8, 0), 8), 8)
        cur = [ref[rows, lanes] for ref in shifted_refs]
        last = [jnp.where(c > 0, ref[prev_rows, lanes], carry_scr[i, :, lanes])[7:8]
                for i, ref in enumerate(shifted_refs)]
        vf = vf_ref[rows, lanes] if has_mix else None
        return cur, last, vf

    def pair_chunk(p, loaded, lw, la, lm, states, state_delay):
        lanes = slice(p * pw, (p + 1) * pw)
        cur, last, vf = loaded
        prow = lambda i: pv[i:i + 1, lanes]

        def tshift(i, mu):
            prev = jnp.where(first_row, last[i], pltpu.roll(cur[i], 1, 0))
            return cur[i] + (prev - cur[i]) * mu

        xr = tshift(0, prow(_P_MU_R))
        xk = tshift(1, prow(_P_MU_K))
        xv = tshift(2, prow(_P_MU_V))
        xz = tshift(3, prow(_P_MU_Z))
        v_shifted = xv

        ld = -_EXP_M_HALF * jax.nn.sigmoid(prow(_P_W0) + lw)
        asig = jax.nn.sigmoid(prow(_P_A0) + la)
        if has_mix:
            xv = xv + (vf - xv) * jax.nn.sigmoid(prow(_P_MV0) + lm)
        kk = xk * prow(_P_KK)
        kp = xk * (1.0 + (asig - 1.0) * prow(_P_KA))
        cum = _dot(tri2, jnp.concatenate(_split_terms(ld, 2), axis=0))
        kk_sq, rk_sum = yield [kk * kk, xr * kp * prow(_P_RK)]
        kkn = kk * lax.rsqrt(jnp.maximum(kk_sq, 1e-24))
        a = -kkn
        b = kkn * asig
        bonus = rk_sum * xv
        e_in = jnp.exp(cum)
        e_out = jnp.exp(-cum)
        w_end = e_in[c_ - 1:c_, :]
        e_end = w_end * e_out
        a_t = a * jnp.exp(cum - ld)
        r_t = xr * e_in
        b_t = b * e_out
        k_t = kp * e_out
        b_c = _bf(b * e_end)
        k_c = _bf(kp * e_end)
        v_st = stack(xv)

        g = lax.dot_general(_bf(jnp.concatenate([a_t, r_t], axis=0)),
                            jnp.concatenate([stack(b_t), stack(k_t)], axis=0), _NT,
                            preferred_element_type=F32)
        a_ab = jnp.where(strict_lower, g[:c_, :st], 0.0)
        a_ak = jnp.where(strict_lower, g[:c_, st:], 0.0)
        a_r = jnp.where(lower, g[c_:, :], 0.0)
        a_rb, a_rk = _bf(a_r[:, :st]), a_r[:, st:]
        yield

        av = _dot(_bf(jnp.concatenate([a_ak, a_rk], axis=0)), v_st)
        akv, x3 = av[:c_], av[c_:]
        t_inv = eye + a_ab
        a_pow = _dot(_bf(a_ab), blockdiag_of(a_ab))
        yield
        span = 2
        while span < c_:
            pow_bd = blockdiag_of(a_pow)
            span *= 2
            if span < c_:
                both = _dot(_bf(jnp.concatenate([t_inv, a_pow], axis=0)), pow_bd)
                t_inv, a_pow = t_inv + both[:c_], both[c_:]
            else:
                t_inv = t_inv + _dot(_bf(t_inv), pow_bd)
            yield

        x1 = _dot(_bf(t_inv), jnp.concatenate([stack(a_t), stack(akv)], axis=1))
        yield
        a_p, u0 = x1[:, :pw], x1[:, pw:]
        x2 = _dot(a_rb, jnp.concatenate([stack(a_p), stack(u0)], axis=1))
        yield
        r_p = r_t + x2[:, :pw]
        y0 = x2[:, pw:] + x3
        for _ in range(state_delay):
            yield

        state = states[p]
        ou = lax.dot_general(_bf(jnp.concatenate([r_p, a_p], axis=0)), _bf(state), _NT,
                             preferred_element_type=F32)
        yield
        o = ou[:c_] + y0
        u = ou[c_:] + u0
        states[p] = state * w_end + jnp.where(
            blockdiag,
            lax.dot_general(_bf(jnp.concatenate([u, xv], axis=0)),
                            jnp.concatenate([b_c, k_c], axis=0), _TN,
                            preferred_element_type=F32), 0.0)

        (sum_o,) = yield [o]
        d = o - sum_o * (1.0 / HEAD_DIM)
        (sum_dd,) = yield [d * d]
        var = sum_dd * (1.0 / HEAD_DIM)
        o = d * lax.rsqrt(var + LNX_EPS) * prow(_P_LNG) + prow(_P_LNB)
        return _bf((o + bonus) * jax.nn.silu(xz)), v_shifted

    step_rows = c_ * RWKV_CHUNKS_PER_STEP

    def lora_up(step):
        r0 = pl.multiple_of(step * step_rows, step_rows)
        cur = sm_ref[pl.ds(r0, step_rows), :]
        prev_rows = pl.ds(pl.multiple_of(jnp.maximum(r0 - 8, 0), 8), 8)
        last = jnp.where(step > 0, sm_ref[prev_rows, :], carry_sm_scr[...])[7:8]
        first = lax.broadcasted_iota(jnp.int32, cur.shape, 0) == 0
        lane_all = lax.broadcasted_iota(jnp.int32, cur.shape, 1)
        prev = jnp.where(first, last, pltpu.roll(cur, 1, 0))
        sms = cur + (prev - cur) * mus
        wl = wl_ref[...]
        in_cols = lambda lo, n: (lane_all >= lo) & (lane_all < lo + n)
        lw = _dot(_bf(jnp.where(in_cols(0, DECAY_LORA), jnp.tanh(sms), 0.0)), wl)
        la = _dot(_bf(jnp.where(in_cols(DECAY_LORA, AAA_LORA), sms, 0.0)), wl)
        lm = (_dot(_bf(jnp.where(in_cols(DECAY_LORA + AAA_LORA, MV_LORA), sms, 0.0)), wl)
              if has_mix else None)
        return lw, la, lm

    pairs = range(wb // pw)

    def chunk_pair_blk(x, j, p):
        return None if x is None else x[j * c_:(j + 1) * c_, p * pw:(p + 1) * pw]

    def body(step, carry):
        states = [state_scr[p] for p in pairs]
        chunks = [step * RWKV_CHUNKS_PER_STEP + j for j in range(RWKV_CHUNKS_PER_STEP)]
        chains = []
        lw, la, lm = lora_up(step)
        for j, c in enumerate(chunks):
            chains += [pair_chunk(p, load_pair(c, p), chunk_pair_blk(lw, j, p),
                                  chunk_pair_blk(la, j, p), chunk_pair_blk(lm, j, p), states,
                                  max(0, 2 - RWKV_STAGGER) * j) for p in pairs]
        first_round = [(i // len(pairs)) * RWKV_STAGGER for i in range(len(chains))]
        results = [None] * len(chains)
        replies = [None] * len(chains)
        round_idx = -1
        while any(r is None for r in results):
            round_idx += 1
            requests = []
            for i, chain in enumerate(chains):
                if results[i] is not None or round_idx < first_round[i]:
                    continue
                try:
                    wanted = chain.send(replies[i])
                except StopIteration as done:
                    results[i] = done.value
                    continue
                replies[i] = None
                if wanted is not None:
                    requests.append((i, wanted))
            if requests:
                sums = segsum(jnp.concatenate([x for _, xs in requests for x in xs], axis=0))
                row = 0
                for i, xs in requests:
                    replies[i] = [sums[row + k * c_:row + (k + 1) * c_] for k in range(len(xs))]
                    row += len(xs) * c_
        for i, (y, v_shifted) in enumerate(results):
            c, p = chunks[i // len(pairs)], i % len(pairs)
            rows = pl.ds(pl.multiple_of(c * c_, c_), c_)
            lanes = slice(p * pw, (p + 1) * pw)
            y_ref[rows, lanes] = y
            if not has_mix:
                vout_ref[rows, lanes] = v_shifted
        for p in pairs:
            state_scr[p] = states[p]
        return carry

    lax.fori_loop(0, ts // (c_ * RWKV_CHUNKS_PER_STEP), body, 0)

    for i, ref in enumerate(shifted_refs):
        carry_scr[i] = ref[ts - 8:ts, :]
    carry_sm_scr[...] = sm_ref[ts - 8:ts, :]


def _rwkv_call(proj3, small3, v_first, pvec, mu_small, w_lora):
    b, s, _ = proj3.shape
    rw = pvec.shape[1]
    wb, ts = min(RWKV_BLOCK_W, rw), min(RWKV_SEQ_TILE, s)
    groups = rw // wb
    has_mix = v_first is not None
    col0 = 2 * rw // wb

    def col_spec(offset):
        return pl.BlockSpec((None, ts, wb), lambda i, g, j: (i, j, offset + g))

    in_specs = [col_spec(col0), col_spec(col0 + groups), col_spec(col0 + 2 * groups),
                col_spec(col0 + 3 * groups),
                pl.BlockSpec((None, ts, SMALL_W), lambda i, g, j: (i, j, 0))]
    args = [proj3, proj3, proj3, proj3, small3]
    if has_mix:
        in_specs.append(col_spec(0))
        args.append(v_first)
    in_specs += [
        pl.BlockSpec((_P_ROWS, wb), lambda i, g, j: (0, g)),
        pl.BlockSpec((1, SMALL_W), lambda i, g, j: (0, 0)),
        pl.BlockSpec((SMALL_W, wb), lambda i, g, j: (0, g)),
    ]
    args += [pvec, mu_small, w_lora]
    out_specs = [col_spec(0)]
    out_shape = [jax.ShapeDtypeStruct((b, s, rw), BF16)]
    if not has_mix:
        out_specs.append(col_spec(0))
        out_shape.append(jax.ShapeDtypeStruct((b, s, rw), F32))
    res = pl.pallas_call(
        functools.partial(_rwkv_kernel, has_mix),
        grid=(b, groups, s // ts),
        in_specs=in_specs,
        out_specs=out_specs,
        out_shape=out_shape,
        scratch_shapes=[
            pltpu.VMEM((wb // PAIR_W, PAIR_W, PAIR_W), F32),
            pltpu.VMEM((4, 8, wb), F32),
            pltpu.VMEM((8, SMALL_W), F32),
        ],
        compiler_params=_params("parallel", "parallel", "arbitrary"),
        name="rwkv_mixer",
    )(*args)
    return (res[0], v_first) if has_mix else (res[0], res[1])


def _merge_kernel(h_ref, ya_ref, yb_ref, wga_ref, wgb_ref, wa_ref, wb_ref, o_ref,
                  wga_scr, wgb_scr, wa_scr, wb_scr):
    @pl.when(pl.program_id(1) == 0)
    def _():
        wga_scr[...] = _bf(wga_ref[...])
        wgb_scr[...] = _bf(wgb_ref[...])
        wa_scr[...] = _bf(wa_ref[...])
        wb_scr[...] = _bf(wb_ref[...])

    h = h_ref[...]
    ga = jax.nn.sigmoid(_dot(h, wga_scr[...]))
    gb = jax.nn.sigmoid(_dot(h, wgb_scr[...]))
    pa = _dot(ya_ref[...], wa_scr[...])
    pb = _dot(yb_ref[...], wb_scr[...])
    o_ref[...] = _bf(ga * pa + gb * pb)


def _merge_call(h, ya, yb, w_in, gate_col0, w_br_a, w_br_b, layer):
    t, d = h.shape
    ka, kb = ya.shape[1], yb.shape[1]
    tm, tn = min(1024, t), 512
    nj = d // tn
    assert gate_col0 % _LANES == 0
    lane_tiles, ga0, gb0 = tn // _LANES, gate_col0 // _LANES, (gate_col0 + d) // _LANES
    return pl.pallas_call(
        _merge_kernel,
        grid=(nj, t // tm),
        in_specs=[
            pl.BlockSpec((tm, d), lambda j, i: (i, 0)),
            pl.BlockSpec((tm, ka), lambda j, i: (i, 0)),
            pl.BlockSpec((tm, kb), lambda j, i: (i, 0)),
            pl.BlockSpec((pl.Squeezed(), pl.Element(d), pl.Element(tn)),
                         lambda j, i: (layer, 0, (ga0 + j * lane_tiles) * _LANES)),
            pl.BlockSpec((pl.Squeezed(), pl.Element(d), pl.Element(tn)),
                         lambda j, i: (layer, 0, (gb0 + j * lane_tiles) * _LANES)),
            pl.BlockSpec((None, ka, tn), lambda j, i: (layer, 0, j)),
            pl.BlockSpec((None, kb, tn), lambda j, i: (layer, 0, j)),
        ],
        out_specs=pl.BlockSpec((tm, tn), lambda j, i: (i, j)),
        out_shape=jax.ShapeDtypeStruct((t, d), BF16),
        scratch_shapes=[pltpu.VMEM((d, tn), BF16), pltpu.VMEM((d, tn), BF16),
                        pltpu.VMEM((ka, tn), BF16), pltpu.VMEM((kb, tn), BF16)],
        compiler_params=_params("parallel", "arbitrary"),
        name="gated_merge",
    )(h, ya, yb, w_in, w_in, w_br_a, w_br_b)


def _out_kernel(m_ref, w_ref, x_ref, gate_ref, g_ref, o_ref, w_scr):
    @pl.when(pl.program_id(0) == 0)
    def _():
        w_scr[...] = _bf(w_ref[...])

    out = _dot(m_ref[...], w_scr[...])
    y = out * lax.rsqrt(jnp.mean(out * out, axis=-1, keepdims=True) + RMS_EPS)
    o_ref[...] = x_ref[...] + gate_ref[...] * (y * g_ref[...])


def _out_call(merged, w_out, layer, xf, gate, g_post, seq):
    t, d = xf.shape
    tm = 512
    per_seq = seq // tm
    return pl.pallas_call(
        _out_kernel,
        grid=(t // tm,),
        in_specs=[
            pl.BlockSpec((tm, d), lambda i: (i, 0)),
            pl.BlockSpec((None, d, d), lambda i: (layer, 0, 0), pipeline_mode=pl.Buffered(1)),
            pl.BlockSpec((tm, d), lambda i: (i, 0)),
            pl.BlockSpec((None, 1, d), lambda i: (i // per_seq, 0, 0)),
            pl.BlockSpec((1, d), lambda i: (0, 0)),
        ],
        out_specs=pl.BlockSpec((tm, d), lambda i: (i, 0)),
        out_shape=jax.ShapeDtypeStruct((t, d), F32),
        scratch_shapes=[pltpu.VMEM((d, d), BF16)],
        compiler_params=_params("arbitrary"),
        name="out_proj",
    )(merged, w_out, xf, gate, g_post)


def kernel(x, c, w_ada, b_ada, g_pre, w_in, w_pool, pool_scale, mu_shift, w_decay_up, w0, w_aaa_up, a0, w_mv_down, mu_mv, w_mv_up, mv0, k_k, k_a, r_k, lnx_g, lnx_b, w_br_a, w_br_b, w_out, g_post):
    bsz, seq, d = x.shape
    depth = w_ada.shape[0]
    pw = pool_scale.shape[1]
    rw = w0.shape[1]
    t = bsz * seq
    n_main = 2 * pw + 4 * rw
    n_lora = DECAY_LORA + AAA_LORA
    n_shift = 4 * rw + n_lora

    xf = x.reshape(t, d)
    c8 = jnp.pad(c, ((0, 8 - bsz), (0, 0)))
    mod = _mod_call(c8, w_ada, b_ada.reshape(depth, 1, 3 * d))

    v_first = None
    for l in range(depth):
        shift = mod[l, :bsz, :d].reshape(bsz, 1, d)
        scale = mod[l, :bsz, d:2 * d].reshape(bsz, 1, d)
        gate = mod[l, :bsz, 2 * d:].reshape(bsz, 1, d)

        small_cols = [w_in[l, :, n_main:n_main + n_lora]]
        lora_rows = [w_decay_up[l], w_aaa_up[l]]
        mu_small = [mu_shift[l, 4 * rw:n_shift]]
        mv0_l = jnp.zeros((rw,), F32)
        if l > 0:
            small_cols.append(w_mv_down[l - 1])
            lora_rows.append(w_mv_up[l - 1])
            mu_small.append(mu_mv[l - 1])
            mv0_l = mv0[l - 1]
        w_small = jnp.concatenate(small_cols, axis=1)
        w_small = jnp.pad(w_small, ((0, 0), (0, SMALL_W - w_small.shape[1])))
        w_lora = jnp.concatenate(lora_rows, axis=0)
        w_lora = _bf(jnp.pad(w_lora, ((0, SMALL_W - w_lora.shape[0]), (0, 0))))
        mu_small = jnp.concatenate(mu_small)
        mu_small = jnp.pad(mu_small, (0, SMALL_W - mu_small.shape[0])).reshape(1, SMALL_W)
        mu4 = mu_shift[l, :4 * rw].reshape(4, rw)
        pvec = jnp.concatenate(
            [mu4, jnp.stack([w0[l], a0[l], mv0_l, k_k[l], k_a[l], r_k[l].reshape(rw),
                             lnx_g[l], lnx_b[l]]),
             jnp.zeros((_P_ROWS - 12, rw), F32)], axis=0)

        h, small = _prenorm_call(xf, g_pre[l].reshape(1, d), scale, shift, w_small, seq)
        proj = _in_proj_call(h, w_in, l, n_main)
        proj3 = proj.reshape(bsz, seq, n_main)
        small3 = small.reshape(bsz, seq, SMALL_W)

        y_a = _pool_call(proj3, _bf(w_pool[l]), pool_scale[l].reshape(1, pw))
        y_b, v_first = _rwkv_call(proj3, small3, v_first, pvec, mu_small, w_lora)

        merged = _merge_call(h, y_a.reshape(t, pw), y_b.reshape(t, rw), w_in,
                             n_main + n_lora, w_br_a, w_br_b, l)
        xf = _out_call(merged, w_out, l, xf, gate, g_post[l].reshape(1, d), seq)
    return xf.reshape(bsz, seq, d)
```
